```python
import math
import jax
import jax.numpy as jnp
from jax import lax
import numpy as np

D_MODEL = 1024
BATCH = 16
SEQ = 2048
DEPTH = 2
DEC_BATCH = 4
DEC_SEQ = 4096
PAST_LEN = 128

RMS_EPS = 1e-6
RET_HEADS = 4
RET_DK = 128
RET_DV = 128
RET_CHUNK = 128
ROPE_BASE = 10000.0
GDN_HEADS = 4
GDN_DK = 128
GDN_DV = 128
GDN_CHUNK = 64
GDN_CONV = 5
HY_WIDTH = 512
HY_SHORT = 3
HY_EMB = 33
HY_FFN = 64
HY_TARGET = 1e-2
HY_FAST_PCT = 0.3
HY_SLOW_PCT = 1.5
HY_FILTER_SCALE = 0.05
FF_DENSE = 2816
N_EXPERTS = 8
TOP_K = 2
FF_EXPERT = 3584
N_DENSE_LAYERS = (DEPTH + 1) // 2
N_MOE_LAYERS = DEPTH // 2

IN_SPLITS = (
    RET_HEADS * RET_DK,
    RET_HEADS * RET_DK,
    RET_HEADS * RET_DV,
    RET_HEADS * RET_DV,
    GDN_HEADS * (2 * GDN_DK + GDN_DV),
    GDN_HEADS * GDN_DV,
    4 * GDN_HEADS,
    3 * HY_WIDTH,
    3 * D_MODEL,
)
D_IN = sum(IN_SPLITS)

kernel_name = 'hybrid_bidir_retention_gdn_hyena_moe'


def rmsnorm(x, g):
    xf = x.astype(jnp.float32)
    y = xf * lax.rsqrt(jnp.mean(xf * xf, axis=-1, keepdims=True) + RMS_EPS)
    return (y * g.astype(jnp.float32)).astype(x.dtype)


def head_group_norm(o):
    mu = jnp.mean(o, axis=-1, keepdims=True)
    var = jnp.mean(jnp.square(o - mu), axis=-1, keepdims=True)
    return (o - mu) * lax.rsqrt(var + RMS_EPS)


def l2norm(x):
    return x * lax.rsqrt(jnp.sum(x * x, axis=-1, keepdims=True) + 1e-6)


def depthwise_conv(x, w):
    k = w.shape[0]
    return lax.conv_general_dilated(
        x, w[:, None, :].astype(x.dtype), window_strides=(1,), padding=[(k // 2, k // 2)],
        dimension_numbers=('NWC', 'WIO', 'NWC'), feature_group_count=x.shape[-1])


def rotary(x):
    L, d = x.shape[1], x.shape[-1]
    half = d // 2
    inv_freq = ROPE_BASE ** (-jnp.arange(half, dtype=jnp.float32) / half)
    ang = jnp.arange(L, dtype=jnp.float32)[:, None] * inv_freq[None, :]
    cos = jnp.cos(ang)[None, :, None, :]
    sin = jnp.sin(ang)[None, :, None, :]
    x1, x2 = x[..., :half], x[..., half:]
    return jnp.concatenate([x1 * cos - x2 * sin, x2 * cos + x1 * sin], axis=-1)


def to_chunks(t, chunk):
    b, l, h = t.shape[:3]
    t = t.reshape((b, l // chunk, chunk, h) + t.shape[3:])
    return jnp.swapaxes(jnp.moveaxis(t, 1, 0), 2, 3)


def from_chunks(t):
    n, b, h, c, d = t.shape
    return jnp.transpose(t, (1, 0, 3, 2, 4)).reshape(b, n * c, h, d)


def chunk_retention(q, k, v, log_gamma, include_diag):
    B, L, H, dk = q.shape
    dv = v.shape[-1]
    C = RET_CHUNK
    idx = jnp.arange(C, dtype=jnp.float32)
    diff = idx[:, None] - idx[None, :]
    mask = (diff >= 0) if include_diag else (diff > 0)
    decay_intra = jnp.where(mask[None], jnp.exp(jnp.where(mask, diff, 0.0)[None] * log_gamma[:, None, None]), 0.0)
    q_inter = jnp.exp((idx + 1.0)[None, :] * log_gamma[:, None])[:, :, None]
    k_state = jnp.exp((C - 1.0 - idx)[None, :] * log_gamma[:, None])[:, :, None]
    chunk_decay = jnp.exp(C * log_gamma)[:, None, None]

    def step(state, blk):
        qb, kb, vb = blk
        s = jnp.einsum('bhid,bhjd->bhij', qb, kb) * decay_intra
        o = jnp.einsum('bhij,bhjv->bhiv', s, vb) + jnp.einsum('bhid,bhdv->bhiv', qb, state) * q_inter
        state = state * chunk_decay + jnp.einsum('bhjd,bhjv->bhdv', kb * k_state, vb)
        return state, o

    state0 = jnp.zeros((B, H, dk, dv), jnp.float32)
    _, o = lax.scan(step, state0, (to_chunks(q, C), to_chunks(k, C), to_chunks(v, C)))
    return from_chunks(o)


def retention_mixer(rq, rk, rv, rg):
    B, L, _ = rq.shape
    f32 = jnp.float32
    q = rotary(rq.astype(f32).reshape(B, L, RET_HEADS, RET_DK))
    k = rotary(rk.astype(f32).reshape(B, L, RET_HEADS, RET_DK)) * (RET_DK ** -0.5)
    v = rv.astype(f32).reshape(B, L, RET_HEADS, RET_DV)
    log_gamma = jnp.log(1.0 - jnp.exp2(-5.0 - jnp.arange(RET_HEADS, dtype=f32)))
    o_fwd = chunk_retention(q, k, v, log_gamma, True)
    o_bwd = jnp.flip(chunk_retention(jnp.flip(q, 1), jnp.flip(k, 1), jnp.flip(v, 1), log_gamma, False), 1)
    o = head_group_norm(o_fwd + o_bwd).reshape(B, L, RET_HEADS * RET_DV)
    return jax.nn.silu(rg.astype(f32)) * o


def chunk_gated_delta(q, k, v, beta, log_g):
    B, L, H, dk = q.shape
    dv = v.shape[-1]
    C = GDN_CHUNK
    qc, kc, vc = to_chunks(q, C), to_chunks(k, C), to_chunks(v, C)
    bc, gc = to_chunks(beta, C), to_chunks(log_g, C)
    cum = jnp.cumsum(gc, axis=-1)
    idx = jnp.arange(C)
    incl = idx[:, None] >= idx[None, :]
    strict = idx[:, None] > idx[None, :]
    decay_mask = jnp.exp(jnp.where(incl, cum[..., :, None] - cum[..., None, :], -jnp.inf))
    kb = kc * bc[..., None]
    vb = vc * bc[..., None]
    a = jnp.where(strict, jnp.einsum('nbhid,nbhjd->nbhij', kb, kc) * decay_mask, 0.0)
    i_plus_a = a + jnp.eye(C, dtype=a.dtype)
    value = lax.linalg.triangular_solve(i_plus_a, vb, left_side=True, lower=True, unit_diagonal=True)
    k_cum = lax.linalg.triangular_solve(i_plus_a, kb * jnp.exp(cum)[..., None], left_side=True, lower=True, unit_diagonal=True)
    attn = jnp.einsum('nbhid,nbhjd->nbhij', qc, kc) * decay_mask
    q_dec = qc * jnp.exp(cum)[..., None]
    k_dec = kc * jnp.exp(cum[..., -1:] - cum)[..., None]
    g_last = jnp.exp(cum[..., -1])[..., None, None]

    def step(S, blk):
        val, kcd, att, qd, kd, gl = blk
        v_new = val - jnp.einsum('bhcd,bhdv->bhcv', kcd, S)
        o = jnp.einsum('bhcd,bhdv->bhcv', qd, S) + jnp.einsum('bhij,bhjv->bhiv', att, v_new)
        S = S * gl + jnp.einsum('bhcd,bhcv->bhdv', kd, v_new)
        return S, o

    S0 = jnp.zeros((B, H, dk, dv), jnp.float32)
    _, o = lax.scan(step, S0, (value, k_cum, attn, q_dec, k_dec, g_last))
    return from_chunks(o)


def gdn_mixer(qkv, gate, ab, conv_w, a_log, dt_bias, onorm):
    B, L, _ = qkv.shape
    f32 = jnp.float32
    qkv = jax.nn.silu(depthwise_conv(qkv, conv_w).astype(f32))
    q, k, v = jnp.split(qkv, [GDN_HEADS * GDN_DK, 2 * GDN_HEADS * GDN_DK], axis=-1)
    q = l2norm(q.reshape(B, L, GDN_HEADS, GDN_DK)) * (GDN_DK ** -0.5)
    k = l2norm(k.reshape(B, L, GDN_HEADS, GDN_DK))
    v = v.reshape(B, L, GDN_HEADS, GDN_DV)
    ab = ab.astype(f32).reshape(B, L, 4, GDN_HEADS)
    log_g = -jnp.exp(a_log.astype(f32)) * jax.nn.softplus(ab[:, :, 0:2] + dt_bias.astype(f32))
    beta = jax.nn.sigmoid(ab[:, :, 2:4])
    o_fwd = chunk_gated_delta(q, k, v, beta[:, :, 0], log_g[:, :, 0])
    o_bwd = jnp.flip(chunk_gated_delta(jnp.flip(q, 1), jnp.flip(k, 1), jnp.flip(v, 1),
                                       jnp.flip(beta[:, :, 1], 1), jnp.flip(log_g[:, :, 1], 1)), 1)
    o = rmsnorm(o_fwd + o_bwd, onorm).reshape(B, L, GDN_HEADS * GDN_DV)
    return o * jax.nn.silu(gate.astype(f32))


def hyena_filter_fft(L, w1, b1, w2, b2, w3, b3, freq, wout, decay):
    f32 = jnp.float32
    t = jnp.arange(L, dtype=f32)[:, None]
    t01 = t / (L - 1)
    bands = (HY_EMB - 1) // 2
    f = jnp.linspace(1e-4, bands - 1, bands, dtype=f32)[None, :]
    w = 2.0 * math.pi * t / L
    z = jnp.concatenate([t01, jnp.cos(f * w), -jnp.sin(f * w)], axis=-1)
    fr = freq.astype(f32)
    h = jnp.sin(fr * (z @ w1.astype(f32) + b1.astype(f32)))
    h = jnp.sin(fr * (h @ w2.astype(f32) + b2.astype(f32)))
    h = jnp.sin(fr * (h @ w3.astype(f32) + b3.astype(f32)))
    h = (h @ wout.astype(f32)).reshape(L, 2, HY_WIDTH)
    h = h * jnp.exp(-t01[:, :, None] * jnp.abs(decay.astype(f32))[None])
    h_fwd, h_bwd = h[:, 0], h[:, 1]
    circ = jnp.concatenate([h_fwd, jnp.zeros((1, HY_WIDTH), f32), h_bwd[:0:-1]], axis=0)
    return jnp.fft.rfft(circ, axis=0)


def hyena_mixer(proj, conv_w, w1, b1, w2, b2, w3, b3, freq, wout, decay, bias):
    B, L, _ = proj.shape
    u = depthwise_conv(proj, conv_w).astype(jnp.float32)
    x0, x1, v = jnp.split(u, 3, axis=-1)
    filt = hyena_filter_fft(L, w1, b1, w2, b2, w3, b3, freq, wout, decay)
    z = v * x1
    conv = jnp.fft.irfft(jnp.fft.rfft(z, n=2 * L, axis=1) * filt[None], n=2 * L, axis=1)[:, :L]
    z = conv + z * bias.astype(jnp.float32)
    return z * x0


def swiglu(u, w_in, w_out):
    a, b = jnp.split(jnp.einsum('...d,df->...f', u, w_in), 2, axis=-1)
    return jnp.einsum('...f,fd->...d', jax.nn.silu(a) * b, w_out)


def moe_swiglu(u, w_router, w_in, w_out):
    B, L, D = u.shape
    t = u.reshape(B * L, D)
    logits = t.astype(jnp.float32) @ w_router.astype(jnp.float32)
    top_vals, top_idx = lax.top_k(logits, TOP_K)
    top_w = jax.nn.softmax(top_vals, axis=-1)
    combine = jnp.sum(jax.nn.one_hot(top_idx, N_EXPERTS, dtype=jnp.float32) * top_w[..., None], axis=1)
    out = jnp.zeros_like(t)
    for e in range(N_EXPERTS):
        out = out + combine[:, e:e + 1].astype(t.dtype) * swiglu(t, w_in[e], w_out[e])
    return out.reshape(B, L, D)


def mixer_sublayer(x, p, i):
    u = rmsnorm(x, p['norm_mix'][i])
    proj = jnp.einsum('bld,de->ble', u, p['w_in'][i])
    rq, rk, rv, rg, gqkv, gg, gab, hy, gates = jnp.split(proj, np.cumsum(IN_SPLITS)[:-1].tolist(), axis=-1)
    y_ret = retention_mixer(rq, rk, rv, rg).astype(x.dtype)
    y_gdn = gdn_mixer(gqkv, gg, gab, p['gdn_conv_w'][i], p['gdn_A_log'][i], p['gdn_dt_bias'][i],
                      p['gdn_onorm'][i]).astype(x.dtype)
    y_hy = hyena_mixer(hy, p['hy_conv_w'][i], p['hy_w1'][i], p['hy_b1'][i], p['hy_w2'][i], p['hy_b2'][i],
                       p['hy_w3'][i], p['hy_b3'][i], p['hy_freq'][i], p['hy_wout'][i], p['hy_decay'][i],
                       p['hy_bias'][i]).astype(x.dtype)
    g_ret, g_gdn, g_hy = jnp.split(jax.nn.sigmoid(gates), 3, axis=-1)
    merged = (g_ret * jnp.einsum('blc,cd->bld', y_ret, p['w_ret_o'][i])
              + g_gdn * jnp.einsum('blc,cd->bld', y_gdn, p['w_gdn_o'][i])
              + g_hy * jnp.einsum('blc,cd->bld', y_hy, p['w_hy_o'][i]))
    return x + jnp.einsum('bld,de->ble', merged, p['w_o'][i])


def trunk(x, p):
    for i in range(DEPTH):
        x = mixer_sublayer(x, p, i)
        u = rmsnorm(x, p['norm_ffn'][i])
        if i % 2 == 0:
            x = x + swiglu(u, p['dense_w_in'][i // 2], p['dense_w_out'][i // 2])
        else:
            x = x + moe_swiglu(u, p['moe_router'][i // 2], p['moe_w_in'][i // 2], p['moe_w_out'][i // 2])
    return rmsnorm(x, p['norm_final'])


def setup_inputs(seed: int = 0) -> dict:
    key = jax.random.key(seed)
    k = jax.random.split(key, 30)
    f32 = jnp.float32

    def nrm(kk, shape, scale):
        return jax.random.normal(kk, shape, f32) * scale

    def gain(kk, shape):
        return 1.0 + 0.02 * jax.random.normal(kk, shape, f32)

    dt = jnp.exp(jax.random.uniform(k[6], (DEPTH, 2, GDN_HEADS), f32, math.log(1e-3), math.log(1e-1)))
    decay_base = jnp.linspace(-math.log(HY_TARGET) / HY_SLOW_PCT, -math.log(HY_TARGET) / HY_FAST_PCT, HY_WIDTH, dtype=f32)
    return {
        'x_prompt': nrm(k[0], (BATCH, SEQ, D_MODEL), 1.0),
        'x_sample': nrm(k[1], (DEC_BATCH, DEC_SEQ, D_MODEL), 1.0),
        'norm_mix': gain(k[2], (DEPTH, D_MODEL)),
        'w_in': nrm(k[3], (DEPTH, D_MODEL, D_IN), D_MODEL ** -0.5),
        'gdn_conv_w': nrm(k[4], (DEPTH, GDN_CONV, GDN_HEADS * (2 * GDN_DK + GDN_DV)), GDN_CONV ** -0.5),
        'gdn_A_log': jnp.log(jax.random.uniform(k[5], (DEPTH, 2, GDN_HEADS), f32, 1.0, 16.0)),
        'gdn_dt_bias': dt + jnp.log(-jnp.expm1(-dt)),
        'gdn_onorm': gain(k[7], (DEPTH, GDN_DV)),
        'hy_conv_w': nrm(k[8], (DEPTH, HY_SHORT, 3 * HY_WIDTH), HY_SHORT ** -0.5),
        'hy_w1': nrm(k[9], (DEPTH, HY_EMB, HY_FFN), HY_EMB ** -0.5),
        'hy_b1': nrm(k[10], (DEPTH, HY_FFN), 0.1),
        'hy_w2': nrm(k[11], (DEPTH, HY_FFN, HY_FFN), HY_FFN ** -0.5),
        'hy_b2': nrm(k[12], (DEPTH, HY_FFN), 0.1),
        'hy_w3': nrm(k[13], (DEPTH, HY_FFN, HY_FFN), HY_FFN ** -0.5),
        'hy_b3': nrm(k[14], (DEPTH, HY_FFN), 0.1),
        'hy_freq': gain(k[15], (DEPTH, HY_FFN)),
        'hy_wout': nrm(k[16], (DEPTH, HY_FFN, 2 * HY_WIDTH), HY_FILTER_SCALE * HY_FFN ** -0.5),
        'hy_decay': decay_base[None, None, :] * gain(k[17], (DEPTH, 2, HY_WIDTH)),
        'hy_bias': nrm(k[18], (DEPTH, HY_WIDTH), 1.0),
        'w_ret_o': nrm(k[19], (DEPTH, RET_HEADS * RET_DV, D_MODEL), (RET_HEADS * RET_DV) ** -0.5),
        'w_gdn_o': nrm(k[20], (DEPTH, GDN_HEADS * GDN_DV, D_MODEL), (GDN_HEADS * GDN_DV) ** -0.5),
        'w_hy_o': nrm(k[21], (DEPTH, HY_WIDTH, D_MODEL), HY_WIDTH ** -0.5),
        'w_o': nrm(k[22], (DEPTH, D_MODEL, D_MODEL), D_MODEL ** -0.5),
        'norm_ffn': gain(k[23], (DEPTH, D_MODEL)),
        'dense_w_in': nrm(k[24], (N_DENSE_LAYERS, D_MODEL, 2 * FF_DENSE), D_MODEL ** -0.5),
        'dense_w_out': nrm(k[25], (N_DENSE_LAYERS, FF_DENSE, D_MODEL), FF_DENSE ** -0.5),
        'moe_router': nrm(k[26], (N_MOE_LAYERS, D_MODEL, N_EXPERTS), D_MODEL ** -0.5),
        'moe_w_in': nrm(k[27], (N_MOE_LAYERS, N_EXPERTS, D_MODEL, 2 * FF_EXPERT), D_MODEL ** -0.5),
        'moe_w_out': nrm(k[28], (N_MOE_LAYERS, N_EXPERTS, FF_EXPERT, D_MODEL), FF_EXPERT ** -0.5),
        'norm_final': gain(k[29], (D_MODEL,)),
    }


def reference(x_prompt, x_sample, norm_mix, w_in, gdn_conv_w, gdn_A_log, gdn_dt_bias, gdn_onorm,
              hy_conv_w, hy_w1, hy_b1, hy_w2, hy_b2, hy_w3, hy_b3, hy_freq, hy_wout, hy_decay, hy_bias,
              w_ret_o, w_gdn_o, w_hy_o, w_o, norm_ffn, dense_w_in, dense_w_out,
              moe_router, moe_w_in, moe_w_out, norm_final):
    p = dict(norm_mix=norm_mix, w_in=w_in, gdn_conv_w=gdn_conv_w, gdn_A_log=gdn_A_log,
             gdn_dt_bias=gdn_dt_bias, gdn_onorm=gdn_onorm, hy_conv_w=hy_conv_w, hy_w1=hy_w1, hy_b1=hy_b1,
             hy_w2=hy_w2, hy_b2=hy_b2, hy_w3=hy_w3, hy_b3=hy_b3, hy_freq=hy_freq, hy_wout=hy_wout,
             hy_decay=hy_decay, hy_bias=hy_bias, w_ret_o=w_ret_o, w_gdn_o=w_gdn_o, w_hy_o=w_hy_o, w_o=w_o,
             norm_ffn=norm_ffn, dense_w_in=dense_w_in, dense_w_out=dense_w_out, moe_router=moe_router,
             moe_w_in=moe_w_in, moe_w_out=moe_w_out, norm_final=norm_final)
    y_prompt = trunk(x_prompt, p)
    y_sample = trunk(x_sample, p)
    return (y_prompt, y_sample)
```

```python
import functools
import math

import jax
import jax.numpy as jnp
import numpy as np
from jax import lax
from jax.experimental import pallas as pl
from jax.experimental.pallas import tpu as pltpu

F32 = jnp.float32
BF16 = jnp.bfloat16

D_MODEL = 1024
DEPTH = 2
RMS_EPS = 1e-6
RET_HEADS, RET_DK, RET_DV, RET_CHUNK = 4, 128, 128, 128
ROPE_BASE = 10000.0
GDN_HEADS, GDN_DK, GDN_DV, GDN_CHUNK, GDN_CONV = 4, 128, 128, 64, 5
HY_WIDTH, HY_SHORT, HY_EMB, HY_FFN = 512, 3, 33, 64
FF_DENSE = 2816
N_EXPERTS, TOP_K, FF_EXPERT = 8, 2, 3584

COL_GATES = 0
COL_RET = 3072
COL_GDN = 5120
COL_HY = 7168
COL_GAB = 8704
D_PROJ = 8832
LANE = 128

VMEM_LIMIT = 56 * 1024 * 1024


def _cparams(sem):
    return pltpu.CompilerParams(dimension_semantics=sem, vmem_limit_bytes=VMEM_LIMIT)


def _rms(x, g):
    return x * lax.rsqrt(jnp.mean(x * x, axis=-1, keepdims=True) + RMS_EPS) * g


def _sigmoid(x):
    return 1.0 / (1.0 + jnp.exp(-x))


def _bdot(a, b):
    return jnp.dot(a.astype(BF16), b, preferred_element_type=F32)


def _proj_body(x_ref, g_ref, w_ref, o_ref, u_ref):
    @pl.when(pl.program_id(1) == 0)
    def _():
        u_ref[...] = _rms(x_ref[...], g_ref[...]).astype(BF16)

    o_ref[...] = jnp.dot(u_ref[...], w_ref[...], preferred_element_type=F32)


def _proj(x, g, w, tm=512, tn=2944):
    t = x.shape[0]
    return pl.pallas_call(
        _proj_body,
        grid=(t // tm, D_PROJ // tn),
        in_specs=[pl.BlockSpec((tm, D_MODEL), lambda i, j: (i, 0)),
                  pl.BlockSpec((1, D_MODEL), lambda i, j: (0, 0)),
                  pl.BlockSpec((D_MODEL, tn), lambda i, j: (0, j))],
        out_specs=pl.BlockSpec((tm, tn), lambda i, j: (i, j)),
        out_shape=jax.ShapeDtypeStruct((t, D_PROJ), F32),
        scratch_shapes=[pltpu.VMEM((tm, D_MODEL), BF16)],
        compiler_params=_cparams(("parallel", "arbitrary")),
        name="in_proj",
    )(x, g, w)


def _merge_body(x_ref, yr_ref, yg_ref, yh_ref, gr_ref, gg_ref, gh_ref, wr_ref, wg_ref, wh_ref, wo_ref, o_ref):
    merged = (_sigmoid(gr_ref[...]) * _bdot(yr_ref[...], wr_ref[...])
              + _sigmoid(gg_ref[...]) * _bdot(yg_ref[...], wg_ref[...])
              + _sigmoid(gh_ref[...]) * _bdot(yh_ref[...], wh_ref[...]))
    o_ref[...] = x_ref[...] + _bdot(merged, wo_ref[...])


def _merge(x, proj, y_ret, y_gdn, y_hy, w_ret, w_gdn, w_hy, w_o, tm=512):
    t = x.shape[0]
    row = lambda i: (i, 0)
    const = lambda i: (0, 0)
    gate = lambda b: pl.BlockSpec((tm, D_MODEL), lambda i: (i, COL_GATES // D_MODEL + b))
    return pl.pallas_call(
        _merge_body,
        grid=(t // tm,),
        in_specs=[pl.BlockSpec((tm, D_MODEL), row),
                  pl.BlockSpec((tm, 512), row), pl.BlockSpec((tm, 512), row), pl.BlockSpec((tm, 512), row),
                  gate(0), gate(1), gate(2),
                  pl.BlockSpec((512, D_MODEL), const), pl.BlockSpec((512, D_MODEL), const),
                  pl.BlockSpec((512, D_MODEL), const), pl.BlockSpec((D_MODEL, D_MODEL), const)],
        out_specs=pl.BlockSpec((tm, D_MODEL), row),
        out_shape=jax.ShapeDtypeStruct((t, D_MODEL), F32),
        compiler_params=_cparams(("parallel",)),
        name="merge",
    )(x, y_ret, y_gdn, y_hy, proj, proj, proj, w_ret, w_gdn, w_hy, w_o)


def _ffn_body(x_ref, g_ref, wa_ref, wb_ref, wo_ref, o_ref, u_ref, acc_ref):
    j = pl.program_id(1)

    @pl.when(j == 0)
    def _():
        u_ref[...] = _rms(x_ref[...], g_ref[...]).astype(BF16)
        acc_ref[...] = jnp.zeros_like(acc_ref)

    u = u_ref[...]
    a = jnp.dot(u, wa_ref[...], preferred_element_type=F32)
    b = jnp.dot(u, wb_ref[...], preferred_element_type=F32)
    acc_ref[...] += _bdot(a * _sigmoid(a) * b, wo_ref[...])

    @pl.when(j == pl.num_programs(1) - 1)
    def _():
        o_ref[...] = x_ref[...] + acc_ref[...]


def _ffn(x, g, w_in, w_out, tm=512, tf=1408):
    t = x.shape[0]
    nf = FF_DENSE // tf
    return pl.pallas_call(
        _ffn_body,
        grid=(t // tm, nf),
        in_specs=[pl.BlockSpec((tm, D_MODEL), lambda i, j: (i, 0)),
                  pl.BlockSpec((1, D_MODEL), lambda i, j: (0, 0)),
                  pl.BlockSpec((D_MODEL, tf), lambda i, j: (0, j)),
                  pl.BlockSpec((D_MODEL, tf), lambda i, j: (0, j + nf)),
                  pl.BlockSpec((tf, D_MODEL), lambda i, j: (j, 0))],
        out_specs=pl.BlockSpec((tm, D_MODEL), lambda i, j: (i, 0)),
        out_shape=jax.ShapeDtypeStruct((t, D_MODEL), F32),
        scratch_shapes=[pltpu.VMEM((tm, D_MODEL), BF16), pltpu.VMEM((tm, D_MODEL), F32)],
        compiler_params=_cparams(("parallel", "arbitrary")),
        name="dense_ffn",
    )(x, g, w_in, w_in, w_out)


def _route_body(x_ref, g_ref, wr_ref, u_ref, idx_ref, wt_ref):
    u = _rms(x_ref[...], g_ref[...])
    u_ref[...] = u.astype(BF16)
    logits = lax.dot_general(wr_ref[...], u, (((1,), (1,)), ((), ())),
                             precision=lax.Precision.HIGHEST, preferred_element_type=F32)
    row = lax.broadcasted_iota(jnp.int32, logits.shape, 0)
    m1 = jnp.max(logits, axis=0, keepdims=True)
    i1 = jnp.min(jnp.where(logits == m1, row, N_EXPERTS), axis=0, keepdims=True)
    rest = jnp.where(row == i1, -jnp.inf, logits)
    m2 = jnp.max(rest, axis=0, keepdims=True)
    i2 = jnp.min(jnp.where(rest == m2, row, N_EXPERTS), axis=0, keepdims=True)
    e = jnp.exp(m2 - m1)
    idx_ref[0:1, :] = i1
    idx_ref[1:2, :] = i2
    wt_ref[0:1, :] = 1.0 / (1.0 + e)
    wt_ref[1:2, :] = e / (1.0 + e)


def _route(x, g, wr_t, tm=512):
    t = x.shape[0]
    return pl.pallas_call(
        _route_body,
        grid=(t // tm,),
        in_specs=[pl.BlockSpec((tm, D_MODEL), lambda i: (i, 0)),
                  pl.BlockSpec((1, D_MODEL), lambda i: (0, 0)),
                  pl.BlockSpec((N_EXPERTS, D_MODEL), lambda i: (0, 0))],
        out_specs=[pl.BlockSpec((tm, D_MODEL), lambda i: (i, 0)),
                   pl.BlockSpec((TOP_K, tm), lambda i: (0, i)),
                   pl.BlockSpec((TOP_K, tm), lambda i: (0, i))],
        out_shape=[jax.ShapeDtypeStruct((t, D_MODEL), BF16),
                   jax.ShapeDtypeStruct((TOP_K, t), jnp.int32),
                   jax.ShapeDtypeStruct((TOP_K, t), F32)],
        compiler_params=_cparams(("parallel",)),
        name="moe_router",
    )(x, g, wr_t)


def _experts_body(te_ref, tv_ref, x_ref, sw_ref, wa_ref, wb_ref, wo_ref, o_ref, acc_ref):
    i, j = pl.program_id(0), pl.program_id(1)

    @pl.when(j == 0)
    def _():
        acc_ref[...] = jnp.zeros_like(acc_ref)

    @pl.when(tv_ref[i] != 0)
    def _():
        x = x_ref[...]
        a = jnp.dot(x, wa_ref[...], preferred_element_type=F32)
        b = jnp.dot(x, wb_ref[...], preferred_element_type=F32)
        acc_ref[...] += _bdot(a * _sigmoid(a) * b, wo_ref[...])

    @pl.when(j == pl.num_programs(1) - 1)
    def _():
        o_ref[...] = acc_ref[...] * sw_ref[...]


def _experts(tile_expert, tile_valid, xs, sw, w_in, w_out, tm, tf=512):
    p = xs.shape[0]
    nf = FF_EXPERT // tf
    grid_spec = pltpu.PrefetchScalarGridSpec(
        num_scalar_prefetch=2,
        grid=(p // tm, nf),
        in_specs=[pl.BlockSpec((tm, D_MODEL), lambda i, j, te, tv: (i, 0)),
                  pl.BlockSpec((tm, 1), lambda i, j, te, tv: (i, 0)),
                  pl.BlockSpec((None, D_MODEL, tf), lambda i, j, te, tv: (te[i], 0, j)),
                  pl.BlockSpec((None, D_MODEL, tf), lambda i, j, te, tv: (te[i], 0, j + nf)),
                  pl.BlockSpec((None, tf, D_MODEL), lambda i, j, te, tv: (te[i], j, 0))],
        out_specs=pl.BlockSpec((tm, D_MODEL), lambda i, j, te, tv: (i, 0)),
        scratch_shapes=[pltpu.VMEM((tm, D_MODEL), F32)],
    )
    return pl.pallas_call(
        _experts_body,
        grid_spec=grid_spec,
        out_shape=jax.ShapeDtypeStruct((p, D_MODEL), F32),
        compiler_params=_cparams(("parallel", "arbitrary")),
        name="moe_experts",
    )(tile_expert, tile_valid, xs, sw, w_in, w_in, w_out)


def _moe(x, g, wr_t, w_in, w_out, tm=512):
    t = x.shape[0]
    u, idx, wts = _route(x, g, wr_t)
    e_flat = idx.reshape(-1)
    onehot = (e_flat[:, None] == jnp.arange(N_EXPERTS, dtype=jnp.int32)[None, :]).astype(jnp.int32)
    csum = jnp.cumsum(onehot, axis=0)
    rank = jnp.sum((csum - onehot) * onehot, axis=1)
    counts = csum[-1]
    padded = ((counts + tm - 1) // tm) * tm
    ends = jnp.cumsum(padded)
    starts = ends - padded
    dest = starts[e_flat] + rank
    p = TOP_K * t + N_EXPERTS * tm
    tok = jnp.tile(jnp.arange(t, dtype=jnp.int32), TOP_K)
    sorted_tok = jnp.zeros((p,), jnp.int32).at[dest].set(tok)
    sorted_w = jnp.zeros((p,), F32).at[dest].set(wts.reshape(-1))
    tile_start = jnp.arange(p // tm, dtype=jnp.int32) * tm
    tile_valid = (tile_start < ends[-1]).astype(jnp.int32)
    tile_expert = jnp.minimum(jnp.sum((tile_start[:, None] >= ends[None, :]).astype(jnp.int32), axis=1),
                              N_EXPERTS - 1)
    last_e = jnp.max(jnp.where(counts > 0, jnp.arange(N_EXPERTS, dtype=jnp.int32), 0))
    tile_expert = jnp.where(tile_valid != 0, tile_expert, last_e)
    xs = jnp.take(u, sorted_tok, axis=0)
    ys = _experts(tile_expert, tile_valid, xs, sorted_w[:, None], w_in, w_out, tm)
    dest2 = dest.reshape(TOP_K, t)
    return x + jnp.take(ys, dest2[0], axis=0) + jnp.take(ys, dest2[1], axis=0)


def _norm_body(x_ref, g_ref, o_ref):
    o_ref[...] = _rms(x_ref[...], g_ref[...])


def _final_norm(x, g, tm=1024):
    t = x.shape[0]
    return pl.pallas_call(
        _norm_body,
        grid=(t // tm,),
        in_specs=[pl.BlockSpec((tm, D_MODEL), lambda i: (i, 0)), pl.BlockSpec((1, D_MODEL), lambda i: (0, 0))],
        out_specs=pl.BlockSpec((tm, D_MODEL), lambda i: (i, 0)),
        out_shape=jax.ShapeDtypeStruct((t, D_MODEL), F32),
        compiler_params=_cparams(("parallel",)),
        name="final_norm",
    )(x, g)


def _to_chunks(t, chunk):
    b, l, h = t.shape[:3]
    t = t.reshape((b, l // chunk, chunk, h) + t.shape[3:])
    return jnp.swapaxes(jnp.moveaxis(t, 1, 0), 2, 3)


def _from_chunks(t):
    n, b, h, c, d = t.shape
    return jnp.transpose(t, (1, 0, 3, 2, 4)).reshape(b, n * c, h, d)


def _rotary(x):
    l, d = x.shape[1], x.shape[-1]
    half = d // 2
    inv_freq = ROPE_BASE ** (-jnp.arange(half, dtype=F32) / half)
    ang = jnp.arange(l, dtype=F32)[:, None] * inv_freq[None, :]
    cos = jnp.cos(ang)[None, :, None, :]
    sin = jnp.sin(ang)[None, :, None, :]
    x1, x2 = x[..., :half], x[..., half:]
    return jnp.concatenate([x1 * cos - x2 * sin, x2 * cos + x1 * sin], axis=-1)


def _chunk_retention(q, k, v, log_gamma, include_diag):
    b, l, h, dk = q.shape
    dv = v.shape[-1]
    c = RET_CHUNK
    idx = jnp.arange(c, dtype=F32)
    diff = idx[:, None] - idx[None, :]
    mask = (diff >= 0) if include_diag else (diff > 0)
    decay_intra = jnp.where(mask[None], jnp.exp(jnp.where(mask, diff, 0.0)[None] * log_gamma[:, None, None]), 0.0)
    q_inter = jnp.exp((idx + 1.0)[None, :] * log_gamma[:, None])[:, :, None]
    k_state = jnp.exp((c - 1.0 - idx)[None, :] * log_gamma[:, None])[:, :, None]
    chunk_decay = jnp.exp(c * log_gamma)[:, None, None]

    def step(state, blk):
        qb, kb, vb = blk
        s = jnp.einsum('bhid,bhjd->bhij', qb, kb) * decay_intra
        o = jnp.einsum('bhij,bhjv->bhiv', s, vb) + jnp.einsum('bhid,bhdv->bhiv', qb, state) * q_inter
        state = state * chunk_decay + jnp.einsum('bhjd,bhjv->bhdv', kb * k_state, vb)
        return state, o

    state0 = jnp.zeros((b, h, dk, dv), F32)
    _, o = lax.scan(step, state0, (_to_chunks(q, c), _to_chunks(k, c), _to_chunks(v, c)))
    return _from_chunks(o)


def _retention_mixer(rq, rk, rv, rg):
    b, l, _ = rq.shape
    q = _rotary(rq.reshape(b, l, RET_HEADS, RET_DK))
    k = _rotary(rk.reshape(b, l, RET_HEADS, RET_DK)) * (RET_DK ** -0.5)
    v = rv.reshape(b, l, RET_HEADS, RET_DV)
    log_gamma = jnp.log(1.0 - jnp.exp2(-5.0 - jnp.arange(RET_HEADS, dtype=F32)))
    o_fwd = _chunk_retention(q, k, v, log_gamma, True)
    o_bwd = jnp.flip(_chunk_retention(jnp.flip(q, 1), jnp.flip(k, 1), jnp.flip(v, 1), log_gamma, False), 1)
    o = o_fwd + o_bwd
    mu = jnp.mean(o, axis=-1, keepdims=True)
    var = jnp.mean(jnp.square(o - mu), axis=-1, keepdims=True)
    o = ((o - mu) * lax.rsqrt(var + RMS_EPS)).reshape(b, l, RET_HEADS * RET_DV)
    return jax.nn.silu(rg) * o


def _depthwise_conv(x, w):
    k = w.shape[0]
    return lax.conv_general_dilated(
        x, w[:, None, :], window_strides=(1,), padding=[(k // 2, k // 2)],
        dimension_numbers=('NWC', 'WIO', 'NWC'), feature_group_count=x.shape[-1])


def _l2norm(x):
    return x * lax.rsqrt(jnp.sum(x * x, axis=-1, keepdims=True) + 1e-6)


def _chunk_gated_delta(q, k, v, beta, log_g):
    b, l, h, dk = q.shape
    dv = v.shape[-1]
    c = GDN_CHUNK
    qc, kc, vc = _to_chunks(q, c), _to_chunks(k, c), _to_chunks(v, c)
    bc, gc = _to_chunks(beta, c), _to_chunks(log_g, c)
    cum = jnp.cumsum(gc, axis=-1)
    idx = jnp.arange(c)
    incl = idx[:, None] >= idx[None, :]
    strict = idx[:, None] > idx[None, :]
    decay_mask = jnp.exp(jnp.where(incl, cum[..., :, None] - cum[..., None, :], -jnp.inf))
    kb = kc * bc[..., None]
    vb = vc * bc[..., None]
    a = jnp.where(strict, jnp.einsum('nbhid,nbhjd->nbhij', kb, kc) * decay_mask, 0.0)
    i_plus_a = a + jnp.eye(c, dtype=a.dtype)
    value = lax.linalg.triangular_solve(i_plus_a, vb, left_side=True, lower=True, unit_diagonal=True)
    k_cum = lax.linalg.triangular_solve(i_plus_a, kb * jnp.exp(cum)[..., None], left_side=True, lower=True,
                                        unit_diagonal=True)
    attn = jnp.einsum('nbhid,nbhjd->nbhij', qc, kc) * decay_mask
    q_dec = qc * jnp.exp(cum)[..., None]
    k_dec = kc * jnp.exp(cum[..., -1:] - cum)[..., None]
    g_last = jnp.exp(cum[..., -1])[..., None, None]

    def step(s, blk):
        val, kcd, att, qd, kd, gl = blk
        v_new = val - jnp.einsum('bhcd,bhdv->bhcv', kcd, s)
        o = jnp.einsum('bhcd,bhdv->bhcv', qd, s) + jnp.einsum('bhij,bhjv->bhiv', att, v_new)
        s = s * gl + jnp.einsum('bhcd,bhcv->bhdv', kd, v_new)
        return s, o

    s0 = jnp.zeros((b, h, dk, dv), F32)
    _, o = lax.scan(step, s0, (value, k_cum, attn, q_dec, k_dec, g_last))
    return _from_chunks(o)


def _gdn_mixer(qkv, gate, ab, conv_w, a_log, dt_bias, onorm):
    b, l, _ = qkv.shape
    qkv = jax.nn.silu(_depthwise_conv(qkv, conv_w))
    q, k, v = jnp.split(qkv, [GDN_HEADS * GDN_DK, 2 * GDN_HEADS * GDN_DK], axis=-1)
    q = _l2norm(q.reshape(b, l, GDN_HEADS, GDN_DK)) * (GDN_DK ** -0.5)
    k = _l2norm(k.reshape(b, l, GDN_HEADS, GDN_DK))
    v = v.reshape(b, l, GDN_HEADS, GDN_DV)
    ab = ab.reshape(b, l, 4, GDN_HEADS)
    log_g = -jnp.exp(a_log) * jax.nn.softplus(ab[:, :, 0:2] + dt_bias)
    beta = jax.nn.sigmoid(ab[:, :, 2:4])
    o_fwd = _chunk_gated_delta(q, k, v, beta[:, :, 0], log_g[:, :, 0])
    o_bwd = jnp.flip(_chunk_gated_delta(jnp.flip(q, 1), jnp.flip(k, 1), jnp.flip(v, 1),
                                        jnp.flip(beta[:, :, 1], 1), jnp.flip(log_g[:, :, 1], 1)), 1)
    o = o_fwd + o_bwd
    o = (o * lax.rsqrt(jnp.mean(o * o, axis=-1, keepdims=True) + RMS_EPS) * onorm).reshape(b, l, GDN_HEADS * GDN_DV)
    return o * jax.nn.silu(gate)


def _hyena_filter_fft(l, w1, b1, w2, b2, w3, b3, freq, wout, decay):
    t = jnp.arange(l, dtype=F32)[:, None]
    t01 = t / (l - 1)
    bands = (HY_EMB - 1) // 2
    f = jnp.linspace(1e-4, bands - 1, bands, dtype=F32)[None, :]
    w = 2.0 * math.pi * t / l
    z = jnp.concatenate([t01, jnp.cos(f * w), -jnp.sin(f * w)], axis=-1)
    h = jnp.sin(freq * (z @ w1 + b1))
    h = jnp.sin(freq * (h @ w2 + b2))
    h = jnp.sin(freq * (h @ w3 + b3))
    h = (h @ wout).reshape(l, 2, HY_WIDTH)
    h = h * jnp.exp(-t01[:, :, None] * jnp.abs(decay)[None])
    h_fwd, h_bwd = h[:, 0], h[:, 1]
    circ = jnp.concatenate([h_fwd, jnp.zeros((1, HY_WIDTH), F32), h_bwd[:0:-1]], axis=0)
    return jnp.fft.rfft(circ, axis=0)


def _hyena_mixer(proj, conv_w, w1, b1, w2, b2, w3, b3, freq, wout, decay, bias):
    b, l, _ = proj.shape
    u = _depthwise_conv(proj, conv_w)
    x0, x1, v = jnp.split(u, 3, axis=-1)
    filt = _hyena_filter_fft(l, w1, b1, w2, b2, w3, b3, freq, wout, decay)
    z = v * x1
    conv = jnp.fft.irfft(jnp.fft.rfft(z, n=2 * l, axis=1) * filt[None], n=2 * l, axis=1)[:, :l]
    z = conv + z * bias
    return z * x0


def _reorder_w_in(w):
    rest, gab, hy, gates = w[:, :4096], w[:, 4096:4112], w[:, 4112:5648], w[:, 5648:]
    pad = jnp.zeros((D_MODEL, D_PROJ - COL_GAB - 16), w.dtype)
    return jnp.concatenate([gates, rest, hy, gab, pad], axis=1).astype(BF16)


def _trunk(x3, p):
    b, l, _ = x3.shape
    t = b * l
    x = x3.reshape(t, D_MODEL)
    for i in range(DEPTH):
        proj = _proj(x, p['norm_mix'][i][None, :], p['w_in_r'][i])
        pr = proj.reshape(b, l, D_PROJ)
        seg = lambda off, n: pr[:, :, off:off + n]
        y_ret = _retention_mixer(seg(COL_RET, 512), seg(COL_RET + 512, 512), seg(COL_RET + 1024, 512),
                                 seg(COL_RET + 1536, 512))
        y_gdn = _gdn_mixer(seg(COL_GDN, 1536), seg(COL_GDN + 1536, 512), seg(COL_GAB, 16), p['gdn_conv_w'][i],
                           p['gdn_A_log'][i], p['gdn_dt_bias'][i], p['gdn_onorm'][i])
        y_hy = _hyena_mixer(seg(COL_HY, 1536), p['hy_conv_w'][i], p['hy_w1'][i], p['hy_b1'][i], p['hy_w2'][i],
                            p['hy_b2'][i], p['hy_w3'][i], p['hy_b3'][i], p['hy_freq'][i], p['hy_wout'][i],
                            p['hy_decay'][i], p['hy_bias'][i])
        x = _merge(x, proj, y_ret.reshape(t, 512), y_gdn.reshape(t, 512), y_hy.reshape(t, 512),
                   p['w_ret_o_b'][i], p['w_gdn_o_b'][i], p['w_hy_o_b'][i], p['w_o_b'][i])
        g = p['norm_ffn'][i][None, :]
        if i % 2 == 0:
            x = _ffn(x, g, p['dense_w_in_b'][i // 2], p['dense_w_out_b'][i // 2])
        else:
            x = _moe(x, g, p['moe_router_t'][i // 2], p['moe_w_in_b'][i // 2], p['moe_w_out_b'][i // 2])
    return _final_norm(x, p['norm_final'][None, :]).reshape(b, l, D_MODEL)


def kernel(x_prompt, x_sample, norm_mix, w_in, gdn_conv_w, gdn_A_log, gdn_dt_bias, gdn_onorm, hy_conv_w, hy_w1, hy_b1, hy_w2, hy_b2, hy_w3, hy_b3, hy_freq, hy_wout, hy_decay, hy_bias, w_ret_o, w_gdn_o, w_hy_o, w_o, norm_ffn, dense_w_in, dense_w_out, moe_router, moe_w_in, moe_w_out, norm_final):
    p = dict(norm_mix=norm_mix, gdn_conv_w=gdn_conv_w, gdn_A_log=gdn_A_log, gdn_dt_bias=gdn_dt_bias,
             gdn_onorm=gdn_onorm, hy_conv_w=hy_conv_w, hy_w1=hy_w1, hy_b1=hy_b1, hy_w2=hy_w2, hy_b2=hy_b2,
             hy_w3=hy_w3, hy_b3=hy_b3, hy_freq=hy_freq, hy_wout=hy_wout, hy_decay=hy_decay, hy_bias=hy_bias,
             norm_ffn=norm_ffn, norm_final=norm_final)
    p['w_in_r'] = jax.vmap(_reorder_w_in)(w_in)
    p['w_ret_o_b'] = w_ret_o.astype(BF16)
    p['w_gdn_o_b'] = w_gdn_o.astype(BF16)
    p['w_hy_o_b'] = w_hy_o.astype(BF16)
    p['w_o_b'] = w_o.astype(BF16)
    p['dense_w_in_b'] = dense_w_in.astype(BF16)
    p['dense_w_out_b'] = dense_w_out.astype(BF16)
    p['moe_router_t'] = jnp.swapaxes(moe_router, 1, 2)
    p['moe_w_in_b'] = moe_w_in.astype(BF16)
    p['moe_w_out_b'] = moe_w_out.astype(BF16)
    return (_trunk(x_prompt, p), _trunk(x_sample, p))
```

```python
import functools
import math

import jax
import jax.numpy as jnp
from jax import lax
from jax.experimental import pallas as pl
from jax.experimental.pallas import tpu as pltpu

F32 = jnp.float32
BF16 = jnp.bfloat16

D_MODEL = 1024
DEPTH = 2
RMS_EPS = 1e-6
RET_HEADS, RET_DK, RET_DV, RET_CHUNK = 4, 128, 128, 128
ROPE_BASE = 10000.0
GDN_HEADS, GDN_DK, GDN_DV, GDN_CHUNK, GDN_CONV = 4, 128, 128, 64, 5
HY_WIDTH, HY_SHORT, HY_EMB, HY_FFN = 512, 3, 33, 64
FF_DENSE = 2816
N_EXPERTS, TOP_K, FF_EXPERT = 8, 2, 3584

COL_GATES = 0
COL_RET = 3072
COL_GDN = 5120
COL_HY = 7168
COL_GAB = 8704
D_PROJ = 8832

VMEM_LIMIT = 56 * 1024 * 1024
HI = lax.Precision.HIGHEST


def _cparams(sem):
    return pltpu.CompilerParams(dimension_semantics=sem, vmem_limit_bytes=VMEM_LIMIT)


def _rms(x, g):
    return x * lax.rsqrt(jnp.mean(x * x, axis=-1, keepdims=True) + RMS_EPS) * g


def _sigmoid(x):
    return 1.0 / (1.0 + jnp.exp(-x))


def _bdot(a, b):
    return jnp.dot(a.astype(BF16), b, preferred_element_type=F32)


def _dot(a, b):
    return jnp.dot(a.astype(BF16), b.astype(BF16), preferred_element_type=F32)


def _dot_nt(a, b):
    return lax.dot_general(a.astype(BF16), b.astype(BF16), (((1,), (1,)), ((), ())), preferred_element_type=F32)


def _proj_body(x_ref, g_ref, w_ref, o_ref, u_ref):
    @pl.when(pl.program_id(1) == 0)
    def _():
        u_ref[...] = _rms(x_ref[...], g_ref[...]).astype(BF16)

    o_ref[...] = jnp.dot(u_ref[...], w_ref[...], preferred_element_type=F32)


def _proj(x, g, w, tm=512, tn=2944):
    t = x.shape[0]
    return pl.pallas_call(
        _proj_body,
        grid=(t // tm, D_PROJ // tn),
        in_specs=[pl.BlockSpec((tm, D_MODEL), lambda i, j: (i, 0)),
                  pl.BlockSpec((1, D_MODEL), lambda i, j: (0, 0)),
                  pl.BlockSpec((D_MODEL, tn), lambda i, j: (0, j))],
        out_specs=pl.BlockSpec((tm, tn), lambda i, j: (i, j)),
        out_shape=jax.ShapeDtypeStruct((t, D_PROJ), F32),
        scratch_shapes=[pltpu.VMEM((tm, D_MODEL), BF16)],
        compiler_params=_cparams(("parallel", "arbitrary")),
        name="in_proj",
    )(x, g, w)


def _merge_body(x_ref, yr_ref, yg_ref, yh_ref, gr_ref, gg_ref, gh_ref, wr_ref, wg_ref, wh_ref, wo_ref, o_ref):
    merged = (_sigmoid(gr_ref[...]) * _bdot(yr_ref[...], wr_ref[...])
              + _sigmoid(gg_ref[...]) * _bdot(yg_ref[...], wg_ref[...])
              + _sigmoid(gh_ref[...]) * _bdot(yh_ref[...], wh_ref[...]))
    o_ref[...] = x_ref[...] + _bdot(merged, wo_ref[...])


def _merge(x, proj, y_ret, y_gdn, y_hy, w_ret, w_gdn, w_hy, w_o, tm=512):
    t = x.shape[0]
    row = lambda i: (i, 0)
    const = lambda i: (0, 0)
    gate = lambda b: pl.BlockSpec((tm, D_MODEL), lambda i: (i, COL_GATES // D_MODEL + b))
    return pl.pallas_call(
        _merge_body,
        grid=(t // tm,),
        in_specs=[pl.BlockSpec((tm, D_MODEL), row),
                  pl.BlockSpec((tm, 512), row), pl.BlockSpec((tm, 512), row), pl.BlockSpec((tm, 512), row),
                  gate(0), gate(1), gate(2),
                  pl.BlockSpec((512, D_MODEL), const), pl.BlockSpec((512, D_MODEL), const),
                  pl.BlockSpec((512, D_MODEL), const), pl.BlockSpec((D_MODEL, D_MODEL), const)],
        out_specs=pl.BlockSpec((tm, D_MODEL), row),
        out_shape=jax.ShapeDtypeStruct((t, D_MODEL), F32),
        compiler_params=_cparams(("parallel",)),
        name="merge",
    )(x, y_ret, y_gdn, y_hy, proj, proj, proj, w_ret, w_gdn, w_hy, w_o)


def _ffn_body(x_ref, g_ref, wa_ref, wb_ref, wo_ref, o_ref, u_ref, acc_ref):
    j = pl.program_id(1)

    @pl.when(j == 0)
    def _():
        u_ref[...] = _rms(x_ref[...], g_ref[...]).astype(BF16)
        acc_ref[...] = jnp.zeros_like(acc_ref)

    u = u_ref[...]
    a = jnp.dot(u, wa_ref[...], preferred_element_type=F32)
    b = jnp.dot(u, wb_ref[...], preferred_element_type=F32)
    acc_ref[...] += _bdot(a * _sigmoid(a) * b, wo_ref[...])

    @pl.when(j == pl.num_programs(1) - 1)
    def _():
        o_ref[...] = x_ref[...] + acc_ref[...]


def _ffn(x, g, w_in, w_out, tm=512, tf=1408):
    t = x.shape[0]
    nf = FF_DENSE // tf
    return pl.pallas_call(
        _ffn_body,
        grid=(t // tm, nf),
        in_specs=[pl.BlockSpec((tm, D_MODEL), lambda i, j: (i, 0)),
                  pl.BlockSpec((1, D_MODEL), lambda i, j: (0, 0)),
                  pl.BlockSpec((D_MODEL, tf), lambda i, j: (0, j)),
                  pl.BlockSpec((D_MODEL, tf), lambda i, j: (0, j + nf)),
                  pl.BlockSpec((tf, D_MODEL), lambda i, j: (j, 0))],
        out_specs=pl.BlockSpec((tm, D_MODEL), lambda i, j: (i, 0)),
        out_shape=jax.ShapeDtypeStruct((t, D_MODEL), F32),
        scratch_shapes=[pltpu.VMEM((tm, D_MODEL), BF16), pltpu.VMEM((tm, D_MODEL), F32)],
        compiler_params=_cparams(("parallel", "arbitrary")),
        name="dense_ffn",
    )(x, g, w_in, w_in, w_out)


def _route_body(x_ref, g_ref, wr_ref, u_ref, idx_ref, wt_ref):
    u = _rms(x_ref[...], g_ref[...])
    u_ref[...] = u.astype(BF16)
    logits = lax.dot_general(wr_ref[...], u, (((1,), (1,)), ((), ())), precision=HI, preferred_element_type=F32)
    row = lax.broadcasted_iota(jnp.int32, logits.shape, 0)
    m1 = jnp.max(logits, axis=0, keepdims=True)
    i1 = jnp.min(jnp.where(logits == m1, row, N_EXPERTS), axis=0, keepdims=True)
    rest = jnp.where(row == i1, -jnp.inf, logits)
    m2 = jnp.max(rest, axis=0, keepdims=True)
    i2 = jnp.min(jnp.where(rest == m2, row, N_EXPERTS), axis=0, keepdims=True)
    e = jnp.exp(m2 - m1)
    idx_ref[0:1, :] = i1
    idx_ref[1:2, :] = i2
    wt_ref[0:1, :] = 1.0 / (1.0 + e)
    wt_ref[1:2, :] = e / (1.0 + e)


def _route(x, g, wr_t, tm=512):
    t = x.shape[0]
    return pl.pallas_call(
        _route_body,
        grid=(t // tm,),
        in_specs=[pl.BlockSpec((tm, D_MODEL), lambda i: (i, 0)),
                  pl.BlockSpec((1, D_MODEL), lambda i: (0, 0)),
                  pl.BlockSpec((N_EXPERTS, D_MODEL), lambda i: (0, 0))],
        out_specs=[pl.BlockSpec((tm, D_MODEL), lambda i: (i, 0)),
                   pl.BlockSpec((TOP_K, tm), lambda i: (0, i)),
                   pl.BlockSpec((TOP_K, tm), lambda i: (0, i))],
        out_shape=[jax.ShapeDtypeStruct((t, D_MODEL), BF16),
                   jax.ShapeDtypeStruct((TOP_K, t), jnp.int32),
                   jax.ShapeDtypeStruct((TOP_K, t), F32)],
        compiler_params=_cparams(("parallel",)),
        name="moe_router",
    )(x, g, wr_t)


def _experts_body(te_ref, tv_ref, x_ref, sw_ref, wa_ref, wb_ref, wo_ref, o_ref, acc_ref):
    i, j = pl.program_id(0), pl.program_id(1)

    @pl.when(j == 0)
    def _():
        acc_ref[...] = jnp.zeros_like(acc_ref)

    @pl.when(tv_ref[i] != 0)
    def _():
        x = x_ref[...]
        a = jnp.dot(x, wa_ref[...], preferred_element_type=F32)
        b = jnp.dot(x, wb_ref[...], preferred_element_type=F32)
        acc_ref[...] += _bdot(a * _sigmoid(a) * b, wo_ref[...])

    @pl.when(j == pl.num_programs(1) - 1)
    def _():
        o_ref[...] = acc_ref[...] * sw_ref[...]


def _experts(tile_expert, tile_valid, xs, sw, w_in, w_out, tm, tf=512):
    p = xs.shape[0]
    nf = FF_EXPERT // tf
    grid_spec = pltpu.PrefetchScalarGridSpec(
        num_scalar_prefetch=2,
        grid=(p // tm, nf),
        in_specs=[pl.BlockSpec((tm, D_MODEL), lambda i, j, te, tv: (i, 0)),
                  pl.BlockSpec((tm, 1), lambda i, j, te, tv: (i, 0)),
                  pl.BlockSpec((None, D_MODEL, tf), lambda i, j, te, tv: (te[i], 0, j)),
                  pl.BlockSpec((None, D_MODEL, tf), lambda i, j, te, tv: (te[i], 0, j + nf)),
                  pl.BlockSpec((None, tf, D_MODEL), lambda i, j, te, tv: (te[i], j, 0))],
        out_specs=pl.BlockSpec((tm, D_MODEL), lambda i, j, te, tv: (i, 0)),
        scratch_shapes=[pltpu.VMEM((tm, D_MODEL), F32)],
    )
    return pl.pallas_call(
        _experts_body,
        grid_spec=grid_spec,
        out_shape=jax.ShapeDtypeStruct((p, D_MODEL), F32),
        compiler_params=_cparams(("parallel", "arbitrary")),
        name="moe_experts",
    )(tile_expert, tile_valid, xs, sw, w_in, w_in, w_out)


def _moe(x, g, wr_t, w_in, w_out, tm=512):
    t = x.shape[0]
    u, idx, wts = _route(x, g, wr_t)
    e_flat = idx.reshape(-1)
    onehot = (e_flat[:, None] == jnp.arange(N_EXPERTS, dtype=jnp.int32)[None, :]).astype(jnp.int32)
    csum = jnp.cumsum(onehot, axis=0)
    rank = jnp.sum((csum - onehot) * onehot, axis=1)
    counts = csum[-1]
    padded = ((counts + tm - 1) // tm) * tm
    ends = jnp.cumsum(padded)
    starts = ends - padded
    dest = starts[e_flat] + rank
    p = TOP_K * t + N_EXPERTS * tm
    tok = jnp.tile(jnp.arange(t, dtype=jnp.int32), TOP_K)
    sorted_tok = jnp.zeros((p,), jnp.int32).at[dest].set(tok)
    sorted_w = jnp.zeros((p,), F32).at[dest].set(wts.reshape(-1))
    tile_start = jnp.arange(p // tm, dtype=jnp.int32) * tm
    tile_valid = (tile_start < ends[-1]).astype(jnp.int32)
    tile_expert = jnp.minimum(jnp.sum((tile_start[:, None] >= ends[None, :]).astype(jnp.int32), axis=1),
                              N_EXPERTS - 1)
    last_e = jnp.max(jnp.where(counts > 0, jnp.arange(N_EXPERTS, dtype=jnp.int32), 0))
    tile_expert = jnp.where(tile_valid != 0, tile_expert, last_e)
    xs = jnp.take(u, sorted_tok, axis=0)
    ys = _experts(tile_expert, tile_valid, xs, sorted_w[:, None], w_in, w_out, tm)
    dest2 = dest.reshape(TOP_K, t)
    return x + jnp.take(ys, dest2[0], axis=0) + jnp.take(ys, dest2[1], axis=0)


def _norm_body(x_ref, g_ref, o_ref):
    o_ref[...] = _rms(x_ref[...], g_ref[...])


def _final_norm(x, g, tm=1024):
    t = x.shape[0]
    return pl.pallas_call(
        _norm_body,
        grid=(t // tm,),
        in_specs=[pl.BlockSpec((tm, D_MODEL), lambda i: (i, 0)), pl.BlockSpec((1, D_MODEL), lambda i: (0, 0))],
        out_specs=pl.BlockSpec((tm, D_MODEL), lambda i: (i, 0)),
        out_shape=jax.ShapeDtypeStruct((t, D_MODEL), F32),
        compiler_params=_cparams(("parallel",)),
        name="final_norm",
    )(x, g)


def _ret_tables(l):
    half = RET_DK // 2
    inv_freq = ROPE_BASE ** (-jnp.arange(half, dtype=F32) / half)
    ang = jnp.arange(l, dtype=F32)[:, None] * inv_freq[None, :]
    cos, sin = jnp.cos(ang), jnp.sin(ang)
    cosf = jnp.concatenate([cos, cos], axis=1)
    sinf = jnp.concatenate([-sin, sin], axis=1)
    c = RET_CHUNK
    log_gamma = jnp.log(1.0 - jnp.exp2(-5.0 - jnp.arange(RET_HEADS, dtype=F32)))[:, None, None]
    idx = jnp.arange(c, dtype=F32)
    dmat = jnp.exp(jnp.abs(idx[:, None] - idx[None, :])[None] * log_gamma)
    col = lambda e: jnp.broadcast_to(jnp.exp(e[None, :, None] * log_gamma), (RET_HEADS, c, 128))
    sc = jnp.stack([col(idx + 1.0), col(c - idx), col(c - 1.0 - idx), col(idx),
                    jnp.broadcast_to(jnp.exp(c * log_gamma), (RET_HEADS, c, 128))], axis=1)
    return cosf, sinf, dmat, sc


def _ret_body(q_ref, k_ref, v_ref, g_ref, cos_ref, sin_ref, dm_ref, sc_ref, o_ref, qs, qd, ks, vs, uf, ub):
    c = RET_CHUNK
    n_chunks = q_ref.shape[0] // c
    q_in, q_out, k_in, k_out = sc_ref[0], sc_ref[1], sc_ref[2], sc_ref[3]
    chunk_decay = sc_ref[4]

    def rot(x, cos, sin):
        return x * cos + pltpu.roll(x, RET_DK // 2, 1) * sin

    def prep(n, carry):
        r = pl.multiple_of(n * c, c)
        rows = pl.ds(r, c)
        cos, sin = cos_ref[rows, :], sin_ref[rows, :]
        q = rot(q_ref[rows, :], cos, sin)
        k = rot(k_ref[rows, :], cos, sin) * (RET_DK ** -0.5)
        v = v_ref[rows, :].astype(BF16)
        qs[rows, :] = q.astype(BF16)
        qd[rows, :] = jnp.concatenate([q * q_in, q * q_out], axis=1).astype(BF16)
        ks[rows, :] = k.astype(BF16)
        vs[rows, :] = v
        uf[n] = jnp.dot((k * k_in).T.astype(BF16), v, preferred_element_type=F32)
        ub[n] = jnp.dot((k * k_out).T.astype(BF16), v, preferred_element_type=F32)
        return carry

    lax.fori_loop(0, n_chunks, prep, 0)

    def fwd_state(n, s):
        u = uf[n]
        uf[n] = s
        return s * chunk_decay + u

    lax.fori_loop(0, n_chunks, fwd_state, jnp.zeros((RET_DK, RET_DV), F32))

    def bwd_state(i, s):
        n = n_chunks - 1 - i
        u = ub[n]
        ub[n] = s
        return s * chunk_decay + u

    lax.fori_loop(0, n_chunks, bwd_state, jnp.zeros((RET_DK, RET_DV), F32))

    def out(n, carry):
        r = pl.multiple_of(n * c, c)
        rows = pl.ds(r, c)
        s = _dot_nt(qs[rows, :], ks[rows, :]) * dm_ref[...]
        state = jnp.concatenate([uf[n], ub[n]], axis=0).astype(BF16)
        o = _dot(s, vs[rows, :]) + jnp.dot(qd[rows, :], state, preferred_element_type=F32)
        mu = jnp.mean(o, axis=-1, keepdims=True)
        d = o - mu
        o = d * lax.rsqrt(jnp.mean(d * d, axis=-1, keepdims=True) + RMS_EPS)
        g = g_ref[rows, :]
        o_ref[rows, :] = g * _sigmoid(g) * o
        return carry

    lax.fori_loop(0, n_chunks, out, 0)


def retention(proj, b, l, col0):
    cosf, sinf, dmat, sc = _ret_tables(l)
    cb = col0 // 128
    nh = RET_HEADS
    c = RET_CHUNK
    seq = lambda off: pl.BlockSpec((l, 128), lambda i, h: (i, cb + off * nh + h))
    const2 = pl.BlockSpec((l, 128), lambda i, h: (0, 0))
    return pl.pallas_call(
        _ret_body,
        grid=(b, nh),
        in_specs=[seq(0), seq(1), seq(2), seq(3), const2, const2,
                  pl.BlockSpec((None, c, c), lambda i, h: (h, 0, 0)),
                  pl.BlockSpec((None, 5, c, 128), lambda i, h: (h, 0, 0, 0))],
        out_specs=pl.BlockSpec((l, 128), lambda i, h: (i, h)),
        out_shape=jax.ShapeDtypeStruct((b * l, nh * RET_DV), F32),
        scratch_shapes=[pltpu.VMEM((l, 128), BF16), pltpu.VMEM((l, 256), BF16), pltpu.VMEM((l, 128), BF16),
                        pltpu.VMEM((l, 128), BF16), pltpu.VMEM((l // c, 128, 128), F32),
                        pltpu.VMEM((l // c, 128, 128), F32)],
        compiler_params=_cparams(("parallel", "parallel")),
        name="retention",
    )(proj, proj, proj, proj, cosf, sinf, dmat, sc)


GDN_ROWS = 256
HALO = 8


def _cumsum_rows(x, reverse):
    n = x.shape[0]
    row = lax.broadcasted_iota(jnp.int32, x.shape, 0)
    s = 1
    while s < n:
        if reverse:
            x = x + jnp.where(row < n - s, pltpu.roll(x, n - s, 0), 0.0)
        else:
            x = x + jnp.where(row >= s, pltpu.roll(x, s, 0), 0.0)
        s *= 2
    return x


def _gdn_chunk(q, k, v, graw, braw, neg_a, dtb, s_ref, reverse):
    c = GDN_CHUNK
    x = graw + dtb
    softplus = jnp.maximum(x, 0.0) + jnp.log(1.0 + jnp.exp(-jnp.abs(x)))
    log_g = neg_a * softplus
    beta = _sigmoid(braw)
    cum = _cumsum_rows(jnp.broadcast_to(log_g, (c, 128)), reverse)
    total = cum[0:1, :] if reverse else cum[c - 1:c, :]
    e_cum = jnp.exp(cum)
    row = lax.broadcasted_iota(jnp.int32, (c, c), 0)
    col = lax.broadcasted_iota(jnp.int32, (c, c), 1)
    cum_sq = cum[:, :c]
    cum_row = jnp.sum(jnp.where(row == col, cum_sq, 0.0), axis=0, keepdims=True)
    incl = (row <= col) if reverse else (row >= col)
    strict = (row < col) if reverse else (row > col)
    decay = jnp.where(incl, jnp.exp(jnp.where(incl, cum_sq - cum_row, 0.0)), 0.0)
    kb = k * beta
    vb = v * beta
    qk = _dot_nt(jnp.concatenate([q, kb], axis=0), k)
    attn = qk[:c] * decay
    a = jnp.where(strict, qk[c:] * decay, 0.0)
    bp = -a
    inv = jnp.where(row == col, 1.0, 0.0) + bp
    for _ in range(5):
        bp = jnp.dot(bp, bp, precision=HI, preferred_element_type=F32)
        inv = inv + jnp.dot(inv, bp, precision=HI, preferred_element_type=F32)
    sol = jnp.dot(inv, jnp.concatenate([vb, kb * e_cum], axis=1), precision=HI, preferred_element_type=F32)
    value, k_cum = sol[:, :GDN_DV], sol[:, GDN_DV:]
    q_dec = q * e_cum
    k_dec = k * jnp.exp(total - cum)
    s = s_ref[...]
    proj_s = _dot(jnp.concatenate([k_cum, q_dec], axis=0), s)
    v_new = value - proj_s[:c]
    o = proj_s[c:] + _dot(attn, v_new)
    zeros = jnp.zeros((c, 128), F32)
    kd_t = jnp.concatenate([k_dec, zeros], axis=0).T
    s_ref[...] = s * jnp.exp(total) + _dot(kd_t, jnp.concatenate([v_new, zeros], axis=0))
    return o


def _gdn_body(alog_ref, dtb_ref, q_ref, k_ref, v_ref, gate_ref, gab_ref, cwq_ref, cwk_ref, cwv_ref, onorm_ref,
              o_ref, pad, qs, ks, vs, of, ob, sf, sb):
    c = GDN_CHUNK
    l = q_ref.shape[0]
    h = pl.program_id(1)
    n_chunks = l // c

    zero_halo = jnp.zeros((HALO, 128), F32)
    for a, src in enumerate((q_ref, k_ref, v_ref)):
        pad[a, 0:HALO, :] = zero_halo
        pad[a, l + HALO:l + 2 * HALO, :] = zero_halo
        pad[a, HALO:l + HALO, :] = src[...]

    win_rows = GDN_ROWS + 2 * HALO

    def conv_block(i, carry):
        s0 = pl.multiple_of(i * GDN_ROWS, GDN_ROWS)
        for a, (cw_ref, dst) in enumerate(((cwq_ref, qs), (cwk_ref, ks), (cwv_ref, vs))):
            win = pad[a, pl.ds(s0, win_rows), :]
            acc = cw_ref[2:3, :] * win
            for j in (0, 1, 3, 4):
                acc = acc + cw_ref[j:j + 1, :] * pltpu.roll(win, (GDN_CONV // 2 - j) % win_rows, 0)
            y = acc[HALO:HALO + GDN_ROWS]
            y = y * _sigmoid(y)
            if a < 2:
                y = y * lax.rsqrt(jnp.sum(y * y, axis=-1, keepdims=True) + 1e-6)
            if a == 0:
                y = y * (GDN_DK ** -0.5)
            dst[pl.ds(s0, GDN_ROWS), :] = y
        return carry

    lax.fori_loop(0, l // GDN_ROWS, conv_block, 0)

    sf[...] = jnp.zeros_like(sf)
    sb[...] = jnp.zeros_like(sb)
    ones = jnp.ones((c, 1), F32)
    neg_a_f = -jnp.exp(ones * alog_ref[0, h])
    neg_a_b = -jnp.exp(ones * alog_ref[1, h])
    dtb_f = dtb_ref[0, h]
    dtb_b = dtb_ref[1, h]

    def scan(i, carry):
        rf = pl.multiple_of(i * c, c)
        rb = pl.multiple_of((n_chunks - 1 - i) * c, c)
        for rows, dst, s_ref, rev, neg_a, dtb in ((pl.ds(rf, c), of, sf, False, neg_a_f, dtb_f),
                                                   (pl.ds(rb, c), ob, sb, True, neg_a_b, dtb_b)):
            ab = gab_ref[rows, :]
            d = 1 if rev else 0
            dst[rows, :] = _gdn_chunk(qs[rows, :], ks[rows, :], vs[rows, :], ab[:, d:d + 1], ab[:, 2 + d:3 + d],
                                      neg_a, dtb, s_ref, rev)
        return carry

    lax.fori_loop(0, n_chunks, scan, 0)

    def fin(i, carry):
        rows = pl.ds(pl.multiple_of(i * GDN_ROWS, GDN_ROWS), GDN_ROWS)
        o = of[rows, :] + ob[rows, :]
        o = o * lax.rsqrt(jnp.mean(o * o, axis=-1, keepdims=True) + RMS_EPS) * onorm_ref[...]
        g = gate_ref[rows, :]
        o_ref[rows, :] = o * (g * _sigmoid(g))
        return carry

    lax.fori_loop(0, l // GDN_ROWS, fin, 0)


def gdn(proj, gabh, conv_w, a_log, dt_bias, onorm, b, l, col0):
    cb = col0 // 128
    nh = GDN_HEADS
    seq = lambda off: pl.BlockSpec((l, 128), lambda i, h, *_: (i, cb + off * nh + h))
    cw = lambda off: pl.BlockSpec((GDN_CONV, 128), lambda i, h, *_: (0, off * nh + h))
    grid_spec = pltpu.PrefetchScalarGridSpec(
        num_scalar_prefetch=2,
        grid=(b, nh),
        in_specs=[seq(0), seq(1), seq(2), seq(3),
                  pl.BlockSpec((None, l, 4), lambda i, h, *_: (h, i, 0)),
                  cw(0), cw(1), cw(2),
                  pl.BlockSpec((1, 128), lambda i, h, *_: (0, 0))],
        out_specs=pl.BlockSpec((l, 128), lambda i, h, *_: (i, h)),
        scratch_shapes=[pltpu.VMEM((3, l + 2 * HALO, 128), F32),
                        pltpu.VMEM((l, 128), F32), pltpu.VMEM((l, 128), F32), pltpu.VMEM((l, 128), F32),
                        pltpu.VMEM((l, 128), F32), pltpu.VMEM((l, 128), F32),
                        pltpu.VMEM((GDN_DK, GDN_DV), F32), pltpu.VMEM((GDN_DK, GDN_DV), F32)],
    )
    return pl.pallas_call(
        _gdn_body,
        grid_spec=grid_spec,
        out_shape=jax.ShapeDtypeStruct((b * l, nh * GDN_DV), F32),
        compiler_params=_cparams(("parallel", "parallel")),
        name="gated_deltanet",
    )(a_log, dt_bias, proj, proj, proj, proj, gabh, conv_w, conv_w, conv_w, onorm[None, :])


HY_N1 = 64
HY_ROWS = 256


def _hyconv_body(x0_ref, x1_ref, v_ref, c0_ref, c1_ref, cv_ref, z_ref, x0c_ref, pad):
    l = x0_ref.shape[0]
    zero_halo = jnp.zeros((HALO, 128), F32)
    for a, src in enumerate((x0_ref, x1_ref, v_ref)):
        pad[a, 0:HALO, :] = zero_halo
        pad[a, l + HALO:l + 2 * HALO, :] = zero_halo
        pad[a, HALO:l + HALO, :] = src[...]
    win_rows = HY_ROWS + 2 * HALO

    def block(i, carry):
        s0 = pl.multiple_of(i * HY_ROWS, HY_ROWS)
        outs = []
        for a, cw_ref in enumerate((c0_ref, c1_ref, cv_ref)):
            win = pad[a, pl.ds(s0, win_rows), :]
            acc = cw_ref[1:2, :] * win
            acc = acc + cw_ref[0:1, :] * pltpu.roll(win, 1, 0)
            acc = acc + cw_ref[2:3, :] * pltpu.roll(win, win_rows - 1, 0)
            outs.append(acc[HALO:HALO + HY_ROWS])
        rows = pl.ds(s0, HY_ROWS)
        x0c_ref[rows, :] = outs[0]
        z_ref[rows, :] = outs[2] * outs[1]
        return carry

    lax.fori_loop(0, l // HY_ROWS, block, 0)


def _hy_conv(proj, conv_w, b, l, col0):
    cb = col0 // 128
    nc = HY_WIDTH // 128
    seq = lambda off: pl.BlockSpec((l, 128), lambda i, c: (i, cb + off * nc + c))
    cw = lambda off: pl.BlockSpec((HY_SHORT, 128), lambda i, c: (0, off * nc + c))
    out = pl.BlockSpec((l, 128), lambda i, c: (i, c))
    return pl.pallas_call(
        _hyconv_body,
        grid=(b, nc),
        in_specs=[seq(0), seq(1), seq(2), cw(0), cw(1), cw(2)],
        out_specs=[out, out],
        out_shape=[jax.ShapeDtypeStruct((b * l, HY_WIDTH), F32), jax.ShapeDtypeStruct((b * l, HY_WIDTH), F32)],
        scratch_shapes=[pltpu.VMEM((3, l + 2 * HALO, 128), F32)],
        compiler_params=_cparams(("parallel", "parallel")),
        name="hyena_short_conv",
    )(proj, proj, proj, conv_w, conv_w, conv_w)


def _hyfilt_body(feat_ref, w1_ref, b1_ref, w2_ref, b2_ref, w3_ref, b3_ref, fr_ref, wo_ref, dec_ref, o_ref, *, l):
    tm = feat_ref.shape[0]
    hdot = lambda a, b_: jnp.dot(a, b_, precision=HI, preferred_element_type=F32)
    feat = feat_ref[...]
    fr = fr_ref[...]
    h = jnp.sin(fr * (hdot(feat, w1_ref[...]) + b1_ref[...]))
    h = jnp.sin(fr * (hdot(h, w2_ref[...]) + b2_ref[...]))
    h = jnp.sin(fr * (hdot(h, w3_ref[...]) + b3_ref[...]))
    h = hdot(h, wo_ref[...]) * jnp.exp(-feat[:, 0:1] * jnp.abs(dec_ref[...]))
    m = pl.program_id(0) * tm + lax.broadcasted_iota(jnp.int32, (tm, 1), 0)
    o_ref[...] = jnp.where(m == l, 0.0, h)


def _hy_filter(l, w1, b1, w2, b2, w3, b3, freq, wout, decay, tm=512):
    n = 2 * l
    m = jnp.arange(n, dtype=jnp.int32)
    t = jnp.where(m < l, m, n - m).astype(F32)[:, None]
    t01 = t / (l - 1)
    bands = (HY_EMB - 1) // 2
    f = jnp.linspace(1e-4, bands - 1, bands, dtype=F32)[None, :]
    w = 2.0 * math.pi * t / l
    feat = jnp.concatenate([t01, jnp.cos(f * w), -jnp.sin(f * w)], axis=-1)
    const = lambda i: (0, 0)
    half = lambda i: (0, (i * tm) // l)
    vec = lambda x: x[None, :]
    return pl.pallas_call(
        functools.partial(_hyfilt_body, l=l),
        grid=(n // tm,),
        in_specs=[pl.BlockSpec((tm, HY_EMB), lambda i: (i, 0)),
                  pl.BlockSpec((HY_EMB, HY_FFN), const), pl.BlockSpec((1, HY_FFN), const),
                  pl.BlockSpec((HY_FFN, HY_FFN), const), pl.BlockSpec((1, HY_FFN), const),
                  pl.BlockSpec((HY_FFN, HY_FFN), const), pl.BlockSpec((1, HY_FFN), const),
                  pl.BlockSpec((1, HY_FFN), const),
                  pl.BlockSpec((HY_FFN, HY_WIDTH), half),
                  pl.BlockSpec((None, 1, HY_WIDTH), lambda i: ((i * tm) // l, 0, 0))],
        out_specs=pl.BlockSpec((tm, HY_WIDTH), lambda i: (i, 0)),
        out_shape=jax.ShapeDtypeStruct((n, HY_WIDTH), F32),
        compiler_params=_cparams(("parallel",)),
        name="hyena_filter",
    )(feat, w1, vec(b1), w2, vec(b2), w3, vec(b3), vec(freq), wout, decay[:, None, :])


def _dft_tables(l):
    n = 2 * l
    n1, n2 = HY_N1, n // HY_N1
    ang = lambda num, den: (2.0 * math.pi / den) * (num % den).astype(F32)
    i1 = jnp.arange(n1, dtype=jnp.int32)
    a1 = ang(i1[:, None] * i1[None, :], n1)
    fwd1 = jnp.concatenate([jnp.cos(a1), -jnp.sin(a1)], axis=0)
    a1h = a1[:, :n1 // 2].T
    inv1 = jnp.concatenate([jnp.cos(a1h), -jnp.sin(a1h)], axis=1) / n
    i2 = jnp.arange(n2, dtype=jnp.int32)
    freq = i1[:, None, None] + n1 * i2[None, :, None]
    a2 = ang(freq * i2[None, None, :], n)
    gr, gi = jnp.cos(a2), -jnp.sin(a2)
    fwd2 = jnp.concatenate([jnp.concatenate([gr, -gi], axis=2), jnp.concatenate([gi, gr], axis=2)], axis=1)
    inv2 = jnp.swapaxes(fwd2, 1, 2)
    return fwd1, inv1, fwd2, inv2


def _dft1_body(x_ref, f_ref, o_ref, *, exact):
    if exact:
        o = jnp.dot(f_ref[...], x_ref[...], precision=HI, preferred_element_type=F32)
    else:
        o = jnp.dot(f_ref[...], x_ref[...].astype(BF16), preferred_element_type=F32)
    o_ref[...] = o.astype(o_ref.dtype)


def _dft1(x3, mat, out_dtype, exact, tn=8192):
    bt, k1, w = x3.shape
    m = mat.shape[0]
    return pl.pallas_call(
        functools.partial(_dft1_body, exact=exact),
        grid=(bt, w // tn),
        in_specs=[pl.BlockSpec((None, k1, tn), lambda i, j: (i, 0, j)),
                  pl.BlockSpec((m, k1), lambda i, j: (0, 0))],
        out_specs=pl.BlockSpec((None, m, tn), lambda i, j: (i, 0, j)),
        out_shape=jax.ShapeDtypeStruct((bt, m, w), out_dtype),
        compiler_params=_cparams(("parallel", "parallel")),
        name="hyena_dft_stage1",
    )(x3, mat)


def _spec2_body(a_ref, g_ref, o_ref):
    n2 = a_ref.shape[1]
    a = a_ref[...].reshape(2 * n2, a_ref.shape[2])
    o = jnp.dot(g_ref[...], a, precision=HI, preferred_element_type=F32)
    o_ref[...] = o.reshape(o_ref.shape)


def _filter_spectrum(circ, fwd1, fwd2, l):
    n1, n2 = HY_N1, 2 * l // HY_N1
    a = _dft1(circ.reshape(1, n1, n2 * HY_WIDTH), fwd1, F32, True)
    a5 = a.reshape(2, n1, n2, HY_WIDTH)
    return pl.pallas_call(
        _spec2_body,
        grid=(n1,),
        in_specs=[pl.BlockSpec((2, None, n2, HY_WIDTH), lambda i: (0, i, 0, 0)),
                  pl.BlockSpec((None, 2 * n2, 2 * n2), lambda i: (i, 0, 0))],
        out_specs=pl.BlockSpec((None, 2, n2, HY_WIDTH), lambda i: (i, 0, 0, 0)),
        out_shape=jax.ShapeDtypeStruct((n1, 2, n2, HY_WIDTH), F32),
        compiler_params=_cparams(("parallel",)),
        name="hyena_filter_spectrum",
    )(a5, fwd2)


def _hymid_body(a_ref, g_ref, gi_ref, h_ref, o_ref):
    bb, _, n2, c = a_ref.shape
    a = jnp.concatenate([a_ref[i].reshape(2 * n2, c) for i in range(bb)], axis=1)
    x = jnp.dot(g_ref[...], a, preferred_element_type=F32)
    hr = jnp.concatenate([h_ref[0]] * bb, axis=1)
    hi = jnp.concatenate([h_ref[1]] * bb, axis=1)
    xr, xi = x[:n2], x[n2:]
    y = jnp.concatenate([xr * hr - xi * hi, xr * hi + xi * hr], axis=0).astype(BF16)
    o = jnp.dot(gi_ref[...], y, preferred_element_type=F32).astype(o_ref.dtype)
    for i in range(bb):
        o_ref[i] = o[:, i * c:(i + 1) * c].reshape(2, n2, c)


def _hy_mid(a5, fwd2, inv2, spec, bb=4):
    b, _, n1, n2, c = a5.shape
    blk = pl.BlockSpec((bb, 2, None, n2, c), lambda f, i: (i, 0, f, 0, 0))
    mat = pl.BlockSpec((None, 2 * n2, 2 * n2), lambda f, i: (f, 0, 0))
    return pl.pallas_call(
        _hymid_body,
        grid=(n1, b // bb),
        in_specs=[blk, mat, mat, pl.BlockSpec((None, 2, n2, c), lambda f, i: (f, 0, 0, 0))],
        out_specs=blk,
        out_shape=jax.ShapeDtypeStruct(a5.shape, BF16),
        compiler_params=_cparams(("parallel", "parallel")),
        name="hyena_spectral_product",
    )(a5, fwd2, inv2, spec)


def _hyfin_body(b_ref, m_ref, z_ref, x0_ref, bias_ref, o_ref):
    y = jnp.dot(m_ref[...], b_ref[...], preferred_element_type=F32)
    z = z_ref[...]
    o_ref[...] = (y + z * bias_ref[...]) * x0_ref[...]


def _hy_final(bq3, inv1, z3, x03, bias, tn=8192):
    b, m2, w = bq3.shape
    k1 = z3.shape[1]
    row = pl.BlockSpec((None, k1, tn), lambda i, j: (i, 0, j))
    bias_t = jnp.tile(bias, tn // HY_WIDTH)[None, :]
    return pl.pallas_call(
        _hyfin_body,
        grid=(b, w // tn),
        in_specs=[pl.BlockSpec((None, m2, tn), lambda i, j: (i, 0, j)),
                  pl.BlockSpec((k1, m2), lambda i, j: (0, 0)), row, row,
                  pl.BlockSpec((1, tn), lambda i, j: (0, 0))],
        out_specs=row,
        out_shape=jax.ShapeDtypeStruct(z3.shape, F32),
        compiler_params=_cparams(("parallel", "parallel")),
        name="hyena_dft_final",
    )(bq3, inv1, z3, x03, bias_t)


def hyena(proj, conv_w, w1, b1, w2, b2, w3, b3, freq, wout, decay, bias, b, l, col0):
    n1, n2, c = HY_N1, 2 * l // HY_N1, HY_WIDTH
    fwd1, inv1, fwd2, inv2 = _dft_tables(l)
    circ = _hy_filter(l, w1, b1, w2, b2, w3, b3, freq, wout, decay)
    spec = _filter_spectrum(circ, fwd1, fwd2, l)
    z, x0c = _hy_conv(proj, conv_w, b, l, col0)
    z3 = z.reshape(b, n1 // 2, n2 * c)
    a = _dft1(z3, fwd1[:, :n1 // 2].astype(BF16), BF16, False)
    bq = _hy_mid(a.reshape(b, 2, n1, n2, c), fwd2.astype(BF16), inv2.astype(BF16), spec)
    out = _hy_final(bq.reshape(b, 2 * n1, n2 * c), inv1.astype(BF16), z3, x0c.reshape(b, n1 // 2, n2 * c), bias)
    return out.reshape(b * l, c)


def _reorder_w_in(w):
    rest, gab, hy, gates = w[:, :4096], w[:, 4096:4112], w[:, 4112:5648], w[:, 5648:]
    pad = jnp.zeros((D_MODEL, D_PROJ - COL_GAB - 16), w.dtype)
    return jnp.concatenate([gates, rest, hy, gab, pad], axis=1).astype(BF16)


def _trunk(x3, p):
    b, l, _ = x3.shape
    t = b * l
    x = x3.reshape(t, D_MODEL)
    for i in range(DEPTH):
        proj = _proj(x, p['norm_mix'][i][None, :], p['w_in_r'][i])
        y_ret = retention(proj, b, l, COL_RET)
        gabh = jnp.transpose(proj[:, COL_GAB:COL_GAB + 4 * GDN_HEADS].reshape(t, 4, GDN_HEADS), (2, 0, 1))
        y_gdn = gdn(proj, gabh, p['gdn_conv_w'][i], p['gdn_A_log'][i], p['gdn_dt_bias'][i], p['gdn_onorm'][i],
                    b, l, COL_GDN)
        y_hy = hyena(proj, p['hy_conv_w'][i], p['hy_w1'][i], p['hy_b1'][i], p['hy_w2'][i], p['hy_b2'][i],
                     p['hy_w3'][i], p['hy_b3'][i], p['hy_freq'][i], p['hy_wout'][i], p['hy_decay'][i],
                     p['hy_bias'][i], b, l, COL_HY)
        x = _merge(x, proj, y_ret, y_gdn, y_hy, p['w_ret_o_b'][i], p['w_gdn_o_b'][i], p['w_hy_o_b'][i], p['w_o_b'][i])
        g = p['norm_ffn'][i][None, :]
        if i % 2 == 0:
            x = _ffn(x, g, p['dense_w_in_b'][i // 2], p['dense_w_out_b'][i // 2])
        else:
            x = _moe(x, g, p['moe_router_t'][i // 2], p['moe_w_in_b'][i // 2], p['moe_w_out_b'][i // 2])
    return _final_norm(x, p['norm_final'][None, :]).reshape(b, l, D_MODEL)


def kernel(x_prompt, x_sample, norm_mix, w_in, gdn_conv_w, gdn_A_log, gdn_dt_bias, gdn_onorm, hy_conv_w, hy_w1, hy_b1, hy_w2, hy_b2, hy_w3, hy_b3, hy_freq, hy_wout, hy_decay, hy_bias, w_ret_o, w_gdn_o, w_hy_o, w_o, norm_ffn, dense_w_in, dense_w_out, moe_router, moe_w_in, moe_w_out, norm_final):
    p = dict(norm_mix=norm_mix, gdn_conv_w=gdn_conv_w, gdn_A_log=gdn_A_log, gdn_dt_bias=gdn_dt_bias,
             gdn_onorm=gdn_onorm, hy_conv_w=hy_conv_w, hy_w1=hy_w1, hy_b1=hy_b1, hy_w2=hy_w2, hy_b2=hy_b2,
             hy_w3=hy_w3, hy_b3=hy_b3, hy_freq=hy_freq, hy_wout=hy_wout, hy_decay=hy_decay, hy_bias=hy_bias,
             norm_ffn=norm_ffn, norm_final=norm_final)
    p['w_in_r'] = jax.vmap(_reorder_w_in)(w_in)
    p['w_ret_o_b'] = w_ret_o.astype(BF16)
    p['w_gdn_o_b'] = w_gdn_o.astype(BF16)
    p['w_hy_o_b'] = w_hy_o.astype(BF16)
    p['w_o_b'] = w_o.astype(BF16)
    p['dense_w_in_b'] = dense_w_in.astype(BF16)
    p['dense_w_out_b'] = dense_w_out.astype(BF16)
    p['moe_router_t'] = jnp.swapaxes(moe_router, 1, 2)
    p['moe_w_in_b'] = moe_w_in.astype(BF16)
    p['moe_w_out_b'] = moe_w_out.astype(BF16)
    return (_trunk(x_prompt, p), _trunk(x_sample, p))
```

```python
import functools
import math

import jax
import jax.numpy as jnp
from jax import lax
from jax.experimental import pallas as pl
from jax.experimental.pallas import tpu as pltpu

F32 = jnp.float32
BF16 = jnp.bfloat16

D_MODEL = 1024
DEPTH = 2
RMS_EPS = 1e-6
RET_HEADS, RET_DK, RET_DV, RET_CHUNK = 4, 128, 128, 128
ROPE_BASE = 10000.0
GDN_HEADS, GDN_DK, GDN_DV, GDN_CHUNK, GDN_CONV = 4, 128, 128, 64, 5
HY_WIDTH, HY_SHORT, HY_EMB, HY_FFN = 512, 3, 33, 64
FF_DENSE = 2816
N_EXPERTS, TOP_K, FF_EXPERT = 8, 2, 3584

COL_GATES = 0
COL_RET = 3072
COL_GDN = 5120
COL_HY = 7168
COL_GAB = 8704
D_PROJ = 8832

VMEM_LIMIT = 56 * 1024 * 1024
HI = lax.Precision.HIGHEST


def _cparams(sem):
    return pltpu.CompilerParams(dimension_semantics=sem, vmem_limit_bytes=VMEM_LIMIT)


def _rms(x, g):
    return x * lax.rsqrt(jnp.mean(x * x, axis=-1, keepdims=True) + RMS_EPS) * g


def _sigmoid(x):
    return 1.0 / (1.0 + jnp.exp(-x))


def _bdot(a, b):
    return jnp.dot(a.astype(BF16), b, preferred_element_type=F32)


def _dot(a, b):
    return jnp.dot(a.astype(BF16), b.astype(BF16), preferred_element_type=F32)


def _dot_nt(a, b):
    return lax.dot_general(a.astype(BF16), b.astype(BF16), (((1,), (1,)), ((), ())), preferred_element_type=F32)


def _proj_body(x_ref, g_ref, w_ref, o_ref, u_ref):
    @pl.when(pl.program_id(1) == 0)
    def _():
        u_ref[...] = _rms(x_ref[...], g_ref[...]).astype(BF16)

    o_ref[...] = jnp.dot(u_ref[...], w_ref[...], preferred_element_type=F32)


def _proj(x, g, w, tm=1024, tn=2944):
    t = x.shape[0]
    return pl.pallas_call(
        _proj_body,
        grid=(t // tm, D_PROJ // tn),
        in_specs=[pl.BlockSpec((tm, D_MODEL), lambda i, j: (i, 0)),
                  pl.BlockSpec((1, D_MODEL), lambda i, j: (0, 0)),
                  pl.BlockSpec((D_MODEL, tn), lambda i, j: (0, j))],
        out_specs=pl.BlockSpec((tm, tn), lambda i, j: (i, j)),
        out_shape=jax.ShapeDtypeStruct((t, D_PROJ), F32),
        scratch_shapes=[pltpu.VMEM((tm, D_MODEL), BF16)],
        compiler_params=_cparams(("parallel", "arbitrary")),
        name="in_proj",
    )(x, g, w)


def _merge_body(x_ref, yr_ref, yg_ref, yh_ref, gr_ref, gg_ref, gh_ref, wr_ref, wg_ref, wh_ref, wo_ref, o_ref):
    merged = (_sigmoid(gr_ref[...]) * _bdot(yr_ref[...], wr_ref[...])
              + _sigmoid(gg_ref[...]) * _bdot(yg_ref[...], wg_ref[...])
              + _sigmoid(gh_ref[...]) * _bdot(yh_ref[...], wh_ref[...]))
    o_ref[...] = x_ref[...] + _bdot(merged, wo_ref[...])


def _merge(x, proj, y_ret, y_gdn, y_hy, w_ret, w_gdn, w_hy, w_o, tm=512):
    t = x.shape[0]
    row = lambda i: (i, 0)
    const = lambda i: (0, 0)
    gate = lambda b: pl.BlockSpec((tm, D_MODEL), lambda i: (i, COL_GATES // D_MODEL + b))
    return pl.pallas_call(
        _merge_body,
        grid=(t // tm,),
        in_specs=[pl.BlockSpec((tm, D_MODEL), row),
                  pl.BlockSpec((tm, 512), row), pl.BlockSpec((tm, 512), row), pl.BlockSpec((tm, 512), row),
                  gate(0), gate(1), gate(2),
                  pl.BlockSpec((512, D_MODEL), const), pl.BlockSpec((512, D_MODEL), const),
                  pl.BlockSpec((512, D_MODEL), const), pl.BlockSpec((D_MODEL, D_MODEL), const)],
        out_specs=pl.BlockSpec((tm, D_MODEL), row),
        out_shape=jax.ShapeDtypeStruct((t, D_MODEL), F32),
        compiler_params=_cparams(("parallel",)),
        name="merge",
    )(x, y_ret, y_gdn, y_hy, proj, proj, proj, w_ret, w_gdn, w_hy, w_o)


def _ffn_body(x_ref, g_ref, wa_ref, wb_ref, wo_ref, o_ref, u_ref, acc_ref):
    j = pl.program_id(1)

    @pl.when(j == 0)
    def _():
        u_ref[...] = _rms(x_ref[...], g_ref[...]).astype(BF16)
        acc_ref[...] = jnp.zeros_like(acc_ref)

    u = u_ref[...]
    a = jnp.dot(u, wa_ref[...], preferred_element_type=F32)
    b = jnp.dot(u, wb_ref[...], preferred_element_type=F32)
    acc_ref[...] += _bdot(a * _sigmoid(a) * b, wo_ref[...])

    @pl.when(j == pl.num_programs(1) - 1)
    def _():
        o_ref[...] = x_ref[...] + acc_ref[...]


def _ffn(x, g, w_in, w_out, tm=512, tf=1408):
    t = x.shape[0]
    nf = FF_DENSE // tf
    return pl.pallas_call(
        _ffn_body,
        grid=(t // tm, nf),
        in_specs=[pl.BlockSpec((tm, D_MODEL), lambda i, j: (i, 0)),
                  pl.BlockSpec((1, D_MODEL), lambda i, j: (0, 0)),
                  pl.BlockSpec((D_MODEL, tf), lambda i, j: (0, j)),
                  pl.BlockSpec((D_MODEL, tf), lambda i, j: (0, j + nf)),
                  pl.BlockSpec((tf, D_MODEL), lambda i, j: (j, 0))],
        out_specs=pl.BlockSpec((tm, D_MODEL), lambda i, j: (i, 0)),
        out_shape=jax.ShapeDtypeStruct((t, D_MODEL), F32),
        scratch_shapes=[pltpu.VMEM((tm, D_MODEL), BF16), pltpu.VMEM((tm, D_MODEL), F32)],
        compiler_params=_cparams(("parallel", "arbitrary")),
        name="dense_ffn",
    )(x, g, w_in, w_in, w_out)


def _route_body(x_ref, g_ref, wr_ref, u_ref, idx_ref, wt_ref):
    u = _rms(x_ref[...], g_ref[...])
    u_ref[...] = u.astype(BF16)
    logits = lax.dot_general(wr_ref[...], u, (((1,), (1,)), ((), ())), precision=HI, preferred_element_type=F32)
    row = lax.broadcasted_iota(jnp.int32, logits.shape, 0)
    m1 = jnp.max(logits, axis=0, keepdims=True)
    i1 = jnp.min(jnp.where(logits == m1, row, N_EXPERTS), axis=0, keepdims=True)
    rest = jnp.where(row == i1, -jnp.inf, logits)
    m2 = jnp.max(rest, axis=0, keepdims=True)
    i2 = jnp.min(jnp.where(rest == m2, row, N_EXPERTS), axis=0, keepdims=True)
    e = jnp.exp(m2 - m1)
    idx_ref[0:1, :] = i1
    idx_ref[1:2, :] = i2
    wt_ref[0:1, :] = 1.0 / (1.0 + e)
    wt_ref[1:2, :] = e / (1.0 + e)


def _route(x, g, wr_t, tm=512):
    t = x.shape[0]
    return pl.pallas_call(
        _route_body,
        grid=(t // tm,),
        in_specs=[pl.BlockSpec((tm, D_MODEL), lambda i: (i, 0)),
                  pl.BlockSpec((1, D_MODEL), lambda i: (0, 0)),
                  pl.BlockSpec((N_EXPERTS, D_MODEL), lambda i: (0, 0))],
        out_specs=[pl.BlockSpec((tm, D_MODEL), lambda i: (i, 0)),
                   pl.BlockSpec((TOP_K, tm), lambda i: (0, i)),
                   pl.BlockSpec((TOP_K, tm), lambda i: (0, i))],
        out_shape=[jax.ShapeDtypeStruct((t, D_MODEL), BF16),
                   jax.ShapeDtypeStruct((TOP_K, t), jnp.int32),
                   jax.ShapeDtypeStruct((TOP_K, t), F32)],
        compiler_params=_cparams(("parallel",)),
        name="moe_router",
    )(x, g, wr_t)


def _experts_body(te_ref, tv_ref, x_ref, sw_ref, wa_ref, wb_ref, wo_ref, o_ref, acc_ref):
    i, j = pl.program_id(0), pl.program_id(1)

    @pl.when(j == 0)
    def _():
        acc_ref[...] = jnp.zeros_like(acc_ref)

    @pl.when(tv_ref[i] != 0)
    def _():
        x = x_ref[...]
        a = jnp.dot(x, wa_ref[...], preferred_element_type=F32)
        b = jnp.dot(x, wb_ref[...], preferred_element_type=F32)
        acc_ref[...] += _bdot(a * _sigmoid(a) * b, wo_ref[...])

    @pl.when(j == pl.num_programs(1) - 1)
    def _():
        o_ref[...] = acc_ref[...] * sw_ref[...]


def _experts(tile_expert, tile_valid, xs, sw, w_in, w_out, tm, tf=512):
    p = xs.shape[0]
    nf = FF_EXPERT // tf
    grid_spec = pltpu.PrefetchScalarGridSpec(
        num_scalar_prefetch=2,
        grid=(p // tm, nf),
        in_specs=[pl.BlockSpec((tm, D_MODEL), lambda i, j, te, tv: (i, 0)),
                  pl.BlockSpec((tm, 1), lambda i, j, te, tv: (i, 0)),
                  pl.BlockSpec((None, D_MODEL, tf), lambda i, j, te, tv: (te[i], 0, j)),
                  pl.BlockSpec((None, D_MODEL, tf), lambda i, j, te, tv: (te[i], 0, j + nf)),
                  pl.BlockSpec((None, tf, D_MODEL), lambda i, j, te, tv: (te[i], j, 0))],
        out_specs=pl.BlockSpec((tm, D_MODEL), lambda i, j, te, tv: (i, 0)),
        scratch_shapes=[pltpu.VMEM((tm, D_MODEL), F32)],
    )
    return pl.pallas_call(
        _experts_body,
        grid_spec=grid_spec,
        out_shape=jax.ShapeDtypeStruct((p, D_MODEL), F32),
        compiler_params=_cparams(("parallel", "arbitrary")),
        name="moe_experts",
    )(tile_expert, tile_valid, xs, sw, w_in, w_in, w_out)


def _moe(x, g, wr_t, w_in, w_out, tm=1024):
    t = x.shape[0]
    u, idx, wts = _route(x, g, wr_t)
    e_flat = idx.reshape(-1)
    onehot = (e_flat[:, None] == jnp.arange(N_EXPERTS, dtype=jnp.int32)[None, :]).astype(jnp.int32)
    csum = jnp.cumsum(onehot, axis=0)
    rank = jnp.sum((csum - onehot) * onehot, axis=1)
    counts = csum[-1]
    padded = ((counts + tm - 1) // tm) * tm
    ends = jnp.cumsum(padded)
    starts = ends - padded
    dest = starts[e_flat] + rank
    p = TOP_K * t + N_EXPERTS * tm
    tok = jnp.tile(jnp.arange(t, dtype=jnp.int32), TOP_K)
    sorted_tok = jnp.zeros((p,), jnp.int32).at[dest].set(tok)
    sorted_w = jnp.zeros((p,), F32).at[dest].set(wts.reshape(-1))
    tile_start = jnp.arange(p // tm, dtype=jnp.int32) * tm
    tile_valid = (tile_start < ends[-1]).astype(jnp.int32)
    tile_expert = jnp.minimum(jnp.sum((tile_start[:, None] >= ends[None, :]).astype(jnp.int32), axis=1),
                              N_EXPERTS - 1)
    last_e = jnp.max(jnp.where(counts > 0, jnp.arange(N_EXPERTS, dtype=jnp.int32), 0))
    tile_expert = jnp.where(tile_valid != 0, tile_expert, last_e)
    xs = jnp.take(u, sorted_tok, axis=0)
    ys = _experts(tile_expert, tile_valid, xs, sorted_w[:, None], w_in, w_out, tm)
    dest2 = dest.reshape(TOP_K, t)
    return x + jnp.take(ys, dest2[0], axis=0) + jnp.take(ys, dest2[1], axis=0)


def _norm_body(x_ref, g_ref, o_ref):
    o_ref[...] = _rms(x_ref[...], g_ref[...])


def _final_norm(x, g, tm=1024):
    t = x.shape[0]
    return pl.pallas_call(
        _norm_body,
        grid=(t // tm,),
        in_specs=[pl.BlockSpec((tm, D_MODEL), lambda i: (i, 0)), pl.BlockSpec((1, D_MODEL), lambda i: (0, 0))],
        out_specs=pl.BlockSpec((tm, D_MODEL), lambda i: (i, 0)),
        out_shape=jax.ShapeDtypeStruct((t, D_MODEL), F32),
        compiler_params=_cparams(("parallel",)),
        name="final_norm",
    )(x, g)


RET_GROUP = 4


def _ret_tables(l):
    half = RET_DK // 2
    inv_freq = ROPE_BASE ** (-jnp.arange(half, dtype=F32) / half)
    ang = jnp.arange(l, dtype=F32)[:, None] * inv_freq[None, :]
    cos, sin = jnp.cos(ang), jnp.sin(ang)
    cosf = jnp.concatenate([cos, cos], axis=1)
    sinf = jnp.concatenate([-sin, sin], axis=1)
    c = RET_CHUNK
    log_gamma = jnp.log(1.0 - jnp.exp2(-5.0 - jnp.arange(RET_HEADS, dtype=F32)))[:, None, None]
    idx = jnp.arange(c, dtype=F32)
    dmat = jnp.exp(jnp.abs(idx[:, None] - idx[None, :])[None] * log_gamma)
    col = lambda e: jnp.broadcast_to(jnp.exp(e[None, :, None] * log_gamma), (RET_HEADS, c, 128))
    sc = jnp.stack([col(idx + 1.0), col(c - idx), col(c - 1.0 - idx), col(idx),
                    jnp.broadcast_to(jnp.exp(c * log_gamma), (RET_HEADS, c, 128))], axis=1)
    return cosf, sinf, dmat, sc


def _ret_body(q_ref, k_ref, v_ref, g_ref, cos_ref, sin_ref, dm_ref, sc_ref, o_ref, qs, qd, ks, vs, uf, ub):
    c = RET_CHUNK
    n_chunks = q_ref.shape[0] // c
    q_in, q_out, k_in, k_out = sc_ref[0], sc_ref[1], sc_ref[2], sc_ref[3]
    chunk_decay = sc_ref[4]

    def rot(x, cos, sin):
        return x * cos + pltpu.roll(x, RET_DK // 2, 1) * sin

    grp = RET_GROUP

    def prep(i, carry):
        ks_f, ks_b, vs_ = [], [], []
        for j in range(grp):
            rows = pl.ds(pl.multiple_of((i * grp + j) * c, c), c)
            cos, sin = cos_ref[rows, :], sin_ref[rows, :]
            q = rot(q_ref[rows, :], cos, sin)
            k = rot(k_ref[rows, :], cos, sin) * (RET_DK ** -0.5)
            v = v_ref[rows, :].astype(BF16)
            qs[rows, :] = q.astype(BF16)
            qd[rows, :] = jnp.concatenate([q * q_in, q * q_out], axis=1).astype(BF16)
            ks[rows, :] = k.astype(BF16)
            vs[rows, :] = v
            ks_f.append((k * k_in).T.astype(BF16))
            ks_b.append((k * k_out).T.astype(BF16))
            vs_.append(v)
        for j in range(grp):
            uf[i * grp + j] = jnp.dot(ks_f[j], vs_[j], preferred_element_type=F32)
            ub[i * grp + j] = jnp.dot(ks_b[j], vs_[j], preferred_element_type=F32)
        return carry

    lax.fori_loop(0, n_chunks // grp, prep, 0)

    def fwd_state(n, s):
        u = uf[n]
        uf[n] = s
        return s * chunk_decay + u

    lax.fori_loop(0, n_chunks, fwd_state, jnp.zeros((RET_DK, RET_DV), F32))

    def bwd_state(i, s):
        n = n_chunks - 1 - i
        u = ub[n]
        ub[n] = s
        return s * chunk_decay + u

    lax.fori_loop(0, n_chunks, bwd_state, jnp.zeros((RET_DK, RET_DV), F32))

    def out(i, carry):
        rows = [pl.ds(pl.multiple_of((i * grp + j) * c, c), c) for j in range(grp)]
        ss = [_dot_nt(qs[r, :], ks[r, :]) for r in rows]
        ss = [(s * dm_ref[...]).astype(BF16) for s in ss]
        states = [jnp.concatenate([uf[i * grp + j], ub[i * grp + j]], axis=0).astype(BF16) for j in range(grp)]
        os_ = [jnp.dot(s, vs[r, :], preferred_element_type=F32) for s, r in zip(ss, rows)]
        os_ = [o + jnp.dot(qd[r, :], st, preferred_element_type=F32) for o, r, st in zip(os_, rows, states)]
        for o, r in zip(os_, rows):
            mu = jnp.mean(o, axis=-1, keepdims=True)
            d = o - mu
            o = d * lax.rsqrt(jnp.mean(d * d, axis=-1, keepdims=True) + RMS_EPS)
            g = g_ref[r, :]
            o_ref[r, :] = g * _sigmoid(g) * o
        return carry

    lax.fori_loop(0, n_chunks // grp, out, 0)


def retention(proj, b, l, col0):
    cosf, sinf, dmat, sc = _ret_tables(l)
    cb = col0 // 128
    nh = RET_HEADS
    c = RET_CHUNK
    seq = lambda off: pl.BlockSpec((l, 128), lambda i, h: (i, cb + off * nh + h))
    const2 = pl.BlockSpec((l, 128), lambda i, h: (0, 0))
    return pl.pallas_call(
        _ret_body,
        grid=(b, nh),
        in_specs=[seq(0), seq(1), seq(2), seq(3), const2, const2,
                  pl.BlockSpec((None, c, c), lambda i, h: (h, 0, 0)),
                  pl.BlockSpec((None, 5, c, 128), lambda i, h: (h, 0, 0, 0))],
        out_specs=pl.BlockSpec((l, 128), lambda i, h: (i, h)),
        out_shape=jax.ShapeDtypeStruct((b * l, nh * RET_DV), F32),
        scratch_shapes=[pltpu.VMEM((l, 128), BF16), pltpu.VMEM((l, 256), BF16), pltpu.VMEM((l, 128), BF16),
                        pltpu.VMEM((l, 128), BF16), pltpu.VMEM((l // c, 128, 128), F32),
                        pltpu.VMEM((l // c, 128, 128), F32)],
        compiler_params=_cparams(("parallel", "parallel")),
        name="retention",
    )(proj, proj, proj, proj, cosf, sinf, dmat, sc)


GDN_ROWS = 256
HALO = 8


def _cumsum_rows(x, reverse):
    n = x.shape[0]
    row = lax.broadcasted_iota(jnp.int32, x.shape, 0)
    s = 1
    while s < n:
        if reverse:
            x = x + jnp.where(row < n - s, pltpu.roll(x, n - s, 0), 0.0)
        else:
            x = x + jnp.where(row >= s, pltpu.roll(x, s, 0), 0.0)
        s *= 2
    return x


def _gdn_local(insts, neg_a, dtb):
    c = GDN_CHUNK
    row = lax.broadcasted_iota(jnp.int32, (c, c), 0)
    col = lax.broadcasted_iota(jnp.int32, (c, c), 1)
    pre = []
    for q, k, v, graw, braw, d in insts:
        reverse = d == 1
        x = graw + dtb[d]
        softplus = jnp.maximum(x, 0.0) + jnp.log(1.0 + jnp.exp(-jnp.abs(x)))
        log_g = neg_a[d] * softplus
        beta = _sigmoid(braw)
        cum = _cumsum_rows(jnp.broadcast_to(log_g, (c, 128)), reverse)
        total = cum[0:1, :] if reverse else cum[c - 1:c, :]
        cum_sq = cum[:, :c]
        cum_row = jnp.sum(jnp.where(row == col, cum_sq, 0.0), axis=0, keepdims=True)
        incl = (row <= col) if reverse else (row >= col)
        strict = (row < col) if reverse else (row > col)
        decay = jnp.where(incl, jnp.exp(jnp.where(incl, cum_sq - cum_row, 0.0)), 0.0)
        pre.append((cum, total, decay, strict, k * beta, v * beta))
    qks = [_dot_nt(jnp.concatenate([q, p[4]], axis=0), k) for (q, k, *_), p in zip(insts, pre)]
    bps = [-jnp.where(p[3], qk[c:] * p[2], 0.0) for qk, p in zip(qks, pre)]
    rs = list(bps)
    for _ in range(5):
        bps = [_dot(bp, bp) for bp in bps]
        rs = [r + bp + _dot(r, bp) for r, bp in zip(rs, bps)]
    rhss = [jnp.concatenate([p[5], p[4] * jnp.exp(p[0])], axis=1) for p in pre]
    sols = [rhs + _dot(r, rhs) for r, rhs in zip(rs, rhss)]
    outs = []
    for (q, k, *_), p, qk, sol in zip(insts, pre, qks, sols):
        cum, total, decay = p[0], p[1], p[2]
        k_dec = k * jnp.exp(total - cum)
        kd_t = jnp.concatenate([k_dec, jnp.zeros((c, 128), F32)], axis=0).T[:, :c]
        outs.append((sol[:, :GDN_DV], jnp.concatenate([sol[:, GDN_DV:], q * jnp.exp(cum)], axis=0),
                     qk[:c] * decay, kd_t, jnp.exp(total)))
    return outs


def _gdn_body(alog_ref, dtb_ref, q_ref, k_ref, v_ref, gate_ref, gab_ref, cwq_ref, cwk_ref, cwv_ref, onorm_ref,
              o_ref, pad, val_s, kq_s, att_s, kdt_s, gl_s, sf, sb):
    c = GDN_CHUNK
    l = q_ref.shape[0]
    h = pl.program_id(1)
    n_chunks = l // c
    per_block = GDN_ROWS // c

    zero_halo = jnp.zeros((HALO, 128), F32)
    for a, src in enumerate((q_ref, k_ref, v_ref)):
        pad[a, 0:HALO, :] = zero_halo
        pad[a, l + HALO:l + 2 * HALO, :] = zero_halo
        pad[a, HALO:l + HALO, :] = src[...]

    ones = jnp.ones((c, 1), F32)
    neg_a = (-jnp.exp(ones * alog_ref[0, h]), -jnp.exp(ones * alog_ref[1, h]))
    dtb = (dtb_ref[0, h], dtb_ref[1, h])
    win_rows = GDN_ROWS + 2 * HALO

    def local_block(i, carry):
        s0 = pl.multiple_of(i * GDN_ROWS, GDN_ROWS)
        ys = []
        for a, cw_ref in enumerate((cwq_ref, cwk_ref, cwv_ref)):
            win = pad[a, pl.ds(s0, win_rows), :]
            acc = cw_ref[2:3, :] * win
            for j in (0, 1, 3, 4):
                acc = acc + cw_ref[j:j + 1, :] * pltpu.roll(win, (GDN_CONV // 2 - j) % win_rows, 0)
            y = acc[HALO:HALO + GDN_ROWS]
            y = y * _sigmoid(y)
            if a < 2:
                y = y * lax.rsqrt(jnp.sum(y * y, axis=-1, keepdims=True) + 1e-6)
            if a == 0:
                y = y * (GDN_DK ** -0.5)
            ys.append(y)
        ab = gab_ref[pl.ds(s0, GDN_ROWS), :]
        insts = []
        for j in range(per_block):
            sl = slice(j * c, (j + 1) * c)
            for d in range(2):
                insts.append((ys[0][sl], ys[1][sl], ys[2][sl], ab[sl, d:d + 1], ab[sl, 2 + d:3 + d], d))
        outs = _gdn_local(insts, neg_a, dtb)
        for idx, (value, kq, attn, kd_t, g_last) in enumerate(outs):
            m = (idx % 2) * n_chunks + i * per_block + idx // 2
            val_s[m] = value
            kq_s[m] = kq.astype(BF16)
            att_s[m] = attn.astype(BF16)
            kdt_s[m] = kd_t.astype(BF16)
            gl_s[m] = jnp.broadcast_to(g_last, (8, 128))
        return carry

    lax.fori_loop(0, l // GDN_ROWS, local_block, 0)

    sf[...] = jnp.zeros_like(sf)
    sb[...] = jnp.zeros_like(sb)

    def scan(i, carry):
        for d, s_ref in ((0, sf), (1, sb)):
            n = i if d == 0 else n_chunks - 1 - i
            m = d * n_chunks + n
            rows = pl.ds(pl.multiple_of(n * c, c) + HALO, c)
            s = s_ref[...]
            proj_s = jnp.dot(kq_s[m], s.astype(BF16), preferred_element_type=F32)
            v_new = (val_s[m] - proj_s[:c]).astype(BF16)
            pad[d, rows, :] = proj_s[c:] + jnp.dot(att_s[m], v_new, preferred_element_type=F32)
            s_ref[...] = s * gl_s[m][0:1, :] + jnp.dot(kdt_s[m], v_new, preferred_element_type=F32)
        return carry

    lax.fori_loop(0, n_chunks, scan, 0)

    def fin(i, carry):
        r0 = pl.multiple_of(i * GDN_ROWS, GDN_ROWS)
        o = pad[0, pl.ds(r0 + HALO, GDN_ROWS), :] + pad[1, pl.ds(r0 + HALO, GDN_ROWS), :]
        o = o * lax.rsqrt(jnp.mean(o * o, axis=-1, keepdims=True) + RMS_EPS) * onorm_ref[...]
        g = gate_ref[pl.ds(r0, GDN_ROWS), :]
        o_ref[pl.ds(r0, GDN_ROWS), :] = o * (g * _sigmoid(g))
        return carry

    lax.fori_loop(0, l // GDN_ROWS, fin, 0)


def gdn(proj, gabh, conv_w, a_log, dt_bias, onorm, b, l, col0):
    cb = col0 // 128
    nh = GDN_HEADS
    nc = l // GDN_CHUNK
    seq = lambda off: pl.BlockSpec((l, 128), lambda i, h, *_: (i, cb + off * nh + h))
    cw = lambda off: pl.BlockSpec((GDN_CONV, 128), lambda i, h, *_: (0, off * nh + h))
    grid_spec = pltpu.PrefetchScalarGridSpec(
        num_scalar_prefetch=2,
        grid=(b, nh),
        in_specs=[seq(0), seq(1), seq(2), seq(3),
                  pl.BlockSpec((None, l, 4), lambda i, h, *_: (h, i, 0)),
                  cw(0), cw(1), cw(2),
                  pl.BlockSpec((1, 128), lambda i, h, *_: (0, 0))],
        out_specs=pl.BlockSpec((l, 128), lambda i, h, *_: (i, h)),
        scratch_shapes=[pltpu.VMEM((3, l + 2 * HALO, 128), F32),
                        pltpu.VMEM((2 * nc, GDN_CHUNK, GDN_DV), F32),
                        pltpu.VMEM((2 * nc, 2 * GDN_CHUNK, GDN_DK), BF16),
                        pltpu.VMEM((2 * nc, GDN_CHUNK, GDN_CHUNK), BF16),
                        pltpu.VMEM((2 * nc, GDN_DK, GDN_CHUNK), BF16),
                        pltpu.VMEM((2 * nc, 8, 128), F32),
                        pltpu.VMEM((GDN_DK, GDN_DV), F32), pltpu.VMEM((GDN_DK, GDN_DV), F32)],
    )
    return pl.pallas_call(
        _gdn_body,
        grid_spec=grid_spec,
        out_shape=jax.ShapeDtypeStruct((b * l, nh * GDN_DV), F32),
        compiler_params=_cparams(("parallel", "parallel")),
        name="gated_deltanet",
    )(a_log, dt_bias, proj, proj, proj, proj, gabh, conv_w, conv_w, conv_w, onorm[None, :])


HY_N1 = 64
HY_ROWS = 256


def _hyconv_body(x0_ref, x1_ref, v_ref, c0_ref, c1_ref, cv_ref, z_ref, x0c_ref, pad):
    l = x0_ref.shape[0]
    zero_halo = jnp.zeros((HALO, 128), F32)
    for a, src in enumerate((x0_ref, x1_ref, v_ref)):
        pad[a, 0:HALO, :] = zero_halo
        pad[a, l + HALO:l + 2 * HALO, :] = zero_halo
        pad[a, HALO:l + HALO, :] = src[...]
    win_rows = HY_ROWS + 2 * HALO

    def block(i, carry):
        s0 = pl.multiple_of(i * HY_ROWS, HY_ROWS)
        outs = []
        for a, cw_ref in enumerate((c0_ref, c1_ref, cv_ref)):
            win = pad[a, pl.ds(s0, win_rows), :]
            acc = cw_ref[1:2, :] * win
            acc = acc + cw_ref[0:1, :] * pltpu.roll(win, 1, 0)
            acc = acc + cw_ref[2:3, :] * pltpu.roll(win, win_rows - 1, 0)
            outs.append(acc[HALO:HALO + HY_ROWS])
        rows = pl.ds(s0, HY_ROWS)
        x0c_ref[rows, :] = outs[0]
        z_ref[rows, :] = outs[2] * outs[1]
        return carry

    lax.fori_loop(0, l // HY_ROWS, block, 0)


def _hy_conv(proj, conv_w, b, l, col0):
    cb = col0 // 128
    nc = HY_WIDTH // 128
    seq = lambda off: pl.BlockSpec((l, 128), lambda i, c: (i, cb + off * nc + c))
    cw = lambda off: pl.BlockSpec((HY_SHORT, 128), lambda i, c: (0, off * nc + c))
    out = pl.BlockSpec((l, 128), lambda i, c: (i, c))
    return pl.pallas_call(
        _hyconv_body,
        grid=(b, nc),
        in_specs=[seq(0), seq(1), seq(2), cw(0), cw(1), cw(2)],
        out_specs=[out, out],
        out_shape=[jax.ShapeDtypeStruct((b * l, HY_WIDTH), F32), jax.ShapeDtypeStruct((b * l, HY_WIDTH), F32)],
        scratch_shapes=[pltpu.VMEM((3, l + 2 * HALO, 128), F32)],
        compiler_params=_cparams(("parallel", "parallel")),
        name="hyena_short_conv",
    )(proj, proj, proj, conv_w, conv_w, conv_w)


def _hyfilt_body(feat_ref, w1_ref, b1_ref, w2_ref, b2_ref, w3_ref, b3_ref, fr_ref, wo_ref, dec_ref, o_ref, *, l):
    tm = feat_ref.shape[0]
    hdot = lambda a, b_: jnp.dot(a, b_, precision=HI, preferred_element_type=F32)
    feat = feat_ref[...]
    fr = fr_ref[...]
    h = jnp.sin(fr * (hdot(feat, w1_ref[...]) + b1_ref[...]))
    h = jnp.sin(fr * (hdot(h, w2_ref[...]) + b2_ref[...]))
    h = jnp.sin(fr * (hdot(h, w3_ref[...]) + b3_ref[...]))
    h = hdot(h, wo_ref[...]) * jnp.exp(-feat[:, 0:1] * jnp.abs(dec_ref[...]))
    m = pl.program_id(0) * tm + lax.broadcasted_iota(jnp.int32, (tm, 1), 0)
    o_ref[...] = jnp.where(m == l, 0.0, h)


def _hy_filter(l, w1, b1, w2, b2, w3, b3, freq, wout, decay, tm=512):
    n = 2 * l
    m = jnp.arange(n, dtype=jnp.int32)
    t = jnp.where(m < l, m, n - m).astype(F32)[:, None]
    t01 = t / (l - 1)
    bands = (HY_EMB - 1) // 2
    f = jnp.linspace(1e-4, bands - 1, bands, dtype=F32)[None, :]
    w = 2.0 * math.pi * t / l
    feat = jnp.concatenate([t01, jnp.cos(f * w), -jnp.sin(f * w)], axis=-1)
    const = lambda i: (0, 0)
    half = lambda i: (0, (i * tm) // l)
    vec = lambda x: x[None, :]
    return pl.pallas_call(
        functools.partial(_hyfilt_body, l=l),
        grid=(n // tm,),
        in_specs=[pl.BlockSpec((tm, HY_EMB), lambda i: (i, 0)),
                  pl.BlockSpec((HY_EMB, HY_FFN), const), pl.BlockSpec((1, HY_FFN), const),
                  pl.BlockSpec((HY_FFN, HY_FFN), const), pl.BlockSpec((1, HY_FFN), const),
                  pl.BlockSpec((HY_FFN, HY_FFN), const), pl.BlockSpec((1, HY_FFN), const),
                  pl.BlockSpec((1, HY_FFN), const),
                  pl.BlockSpec((HY_FFN, HY_WIDTH), half),
                  pl.BlockSpec((None, 1, HY_WIDTH), lambda i: ((i * tm) // l, 0, 0))],
        out_specs=pl.BlockSpec((tm, HY_WIDTH), lambda i: (i, 0)),
        out_shape=jax.ShapeDtypeStruct((n, HY_WIDTH), F32),
        compiler_params=_cparams(("parallel",)),
        name="hyena_filter",
    )(feat, w1, vec(b1), w2, vec(b2), w3, vec(b3), vec(freq), wout, decay[:, None, :])


def _dft_tables(l):
    n = 2 * l
    n1, n2 = HY_N1, n // HY_N1
    ang = lambda num, den: (2.0 * math.pi / den) * (num % den).astype(F32)
    i1 = jnp.arange(n1, dtype=jnp.int32)
    a1 = ang(i1[:, None] * i1[None, :], n1)
    fwd1 = jnp.concatenate([jnp.cos(a1), -jnp.sin(a1)], axis=0)
    a1h = a1[:, :n1 // 2].T
    inv1 = jnp.concatenate([jnp.cos(a1h), -jnp.sin(a1h)], axis=1) / n
    i2 = jnp.arange(n2, dtype=jnp.int32)
    freq = i1[:, None, None] + n1 * i2[None, :, None]
    a2 = ang(freq * i2[None, None, :], n)
    gr, gi = jnp.cos(a2), -jnp.sin(a2)
    fwd2 = jnp.concatenate([jnp.concatenate([gr, -gi], axis=2), jnp.concatenate([gi, gr], axis=2)], axis=1)
    inv2 = jnp.swapaxes(fwd2, 1, 2)
    return fwd1, inv1, fwd2, inv2


HY_TR = 8


def _dft1_body(x_ref, f_ref, o_ref, *, exact):
    n1 = o_ref.shape[1]
    for r in range(x_ref.shape[1]):
        x = x_ref[:, r, :]
        if exact:
            o = jnp.dot(f_ref[...], x, precision=HI, preferred_element_type=F32)
        else:
            o = jnp.dot(f_ref[...], x.astype(BF16), preferred_element_type=F32)
        o_ref[0, :, r, :] = o[:n1].astype(o_ref.dtype)
        o_ref[1, :, r, :] = o[n1:].astype(o_ref.dtype)


def _dft1(x4, mat, out_dtype, exact):
    bt, k1, n2, c = x4.shape
    n1 = mat.shape[0] // 2
    return pl.pallas_call(
        functools.partial(_dft1_body, exact=exact),
        grid=(bt, n2 // HY_TR),
        in_specs=[pl.BlockSpec((None, k1, HY_TR, c), lambda i, j: (i, 0, j, 0)),
                  pl.BlockSpec((2 * n1, k1), lambda i, j: (0, 0))],
        out_specs=pl.BlockSpec((None, 2, n1, HY_TR, c), lambda i, j: (i, 0, 0, j, 0)),
        out_shape=jax.ShapeDtypeStruct((bt, 2, n1, n2, c), out_dtype),
        compiler_params=_cparams(("parallel", "parallel")),
        name="hyena_dft_stage1",
    )(x4, mat)


def _spec2_body(a_ref, g_ref, o_ref):
    n2 = a_ref.shape[1]
    a = a_ref[...].reshape(2 * n2, a_ref.shape[2])
    o = jnp.dot(g_ref[...], a, precision=HI, preferred_element_type=F32)
    o_ref[...] = o.reshape(o_ref.shape)


def _filter_spectrum(circ, fwd1, fwd2, l):
    n1, n2 = HY_N1, 2 * l // HY_N1
    a5 = _dft1(circ.reshape(1, n1, n2, HY_WIDTH), fwd1, F32, True)[0]
    return pl.pallas_call(
        _spec2_body,
        grid=(n1,),
        in_specs=[pl.BlockSpec((2, None, n2, HY_WIDTH), lambda i: (0, i, 0, 0)),
                  pl.BlockSpec((None, 2 * n2, 2 * n2), lambda i: (i, 0, 0))],
        out_specs=pl.BlockSpec((None, 2, n2, HY_WIDTH), lambda i: (i, 0, 0, 0)),
        out_shape=jax.ShapeDtypeStruct((n1, 2, n2, HY_WIDTH), F32),
        compiler_params=_cparams(("parallel",)),
        name="hyena_filter_spectrum",
    )(a5, fwd2)


def _hymid_body(a_ref, g_ref, gi_ref, h_ref, o_ref):
    bb, _, n2, c = a_ref.shape
    a = jnp.concatenate([a_ref[i].reshape(2 * n2, c) for i in range(bb)], axis=1)
    x = jnp.dot(g_ref[...], a, preferred_element_type=F32)
    hr = jnp.concatenate([h_ref[0]] * bb, axis=1)
    hi = jnp.concatenate([h_ref[1]] * bb, axis=1)
    xr, xi = x[:n2], x[n2:]
    y = jnp.concatenate([xr * hr - xi * hi, xr * hi + xi * hr], axis=0).astype(BF16)
    o = jnp.dot(gi_ref[...], y, preferred_element_type=F32).astype(o_ref.dtype)
    for i in range(bb):
        o_ref[i] = o[:, i * c:(i + 1) * c].reshape(2, n2, c)


def _hy_mid(a5, fwd2, inv2, spec, bb=4):
    b, _, n1, n2, c = a5.shape
    blk = pl.BlockSpec((bb, 2, None, n2, c), lambda f, i: (i, 0, f, 0, 0))
    mat = pl.BlockSpec((None, 2 * n2, 2 * n2), lambda f, i: (f, 0, 0))
    return pl.pallas_call(
        _hymid_body,
        grid=(n1, b // bb),
        in_specs=[blk, mat, mat, pl.BlockSpec((None, 2, n2, c), lambda f, i: (f, 0, 0, 0))],
        out_specs=blk,
        out_shape=jax.ShapeDtypeStruct(a5.shape, BF16),
        compiler_params=_cparams(("parallel", "parallel")),
        name="hyena_spectral_product",
    )(a5, fwd2, inv2, spec)


def _hyfin_body(b_ref, m_ref, z_ref, x0_ref, bias_ref, o_ref):
    for r in range(z_ref.shape[1]):
        rhs = jnp.concatenate([b_ref[0, :, r, :], b_ref[1, :, r, :]], axis=0)
        y = jnp.dot(m_ref[...], rhs, preferred_element_type=F32)
        o_ref[:, r, :] = (y + z_ref[:, r, :] * bias_ref[...]) * x0_ref[:, r, :]


def _hy_final(bq5, inv1, z4, x04, bias):
    b, _, n1, n2, c = bq5.shape
    k1 = z4.shape[1]
    row = pl.BlockSpec((None, k1, HY_TR, c), lambda i, j: (i, 0, j, 0))
    return pl.pallas_call(
        _hyfin_body,
        grid=(b, n2 // HY_TR),
        in_specs=[pl.BlockSpec((None, 2, n1, HY_TR, c), lambda i, j: (i, 0, 0, j, 0)),
                  pl.BlockSpec((k1, 2 * n1), lambda i, j: (0, 0)), row, row,
                  pl.BlockSpec((1, c), lambda i, j: (0, 0))],
        out_specs=row,
        out_shape=jax.ShapeDtypeStruct(z4.shape, F32),
        compiler_params=_cparams(("parallel", "parallel")),
        name="hyena_dft_final",
    )(bq5, inv1, z4, x04, bias[None, :])


def hyena(proj, conv_w, w1, b1, w2, b2, w3, b3, freq, wout, decay, bias, b, l, col0):
    n1, n2, c = HY_N1, 2 * l // HY_N1, HY_WIDTH
    fwd1, inv1, fwd2, inv2 = _dft_tables(l)
    circ = _hy_filter(l, w1, b1, w2, b2, w3, b3, freq, wout, decay)
    spec = _filter_spectrum(circ, fwd1, fwd2, l)
    z, x0c = _hy_conv(proj, conv_w, b, l, col0)
    z4 = z.reshape(b, n1 // 2, n2, c)
    a5 = _dft1(z4, fwd1[:, :n1 // 2].astype(BF16), BF16, False)
    bq5 = _hy_mid(a5, fwd2.astype(BF16), inv2.astype(BF16), spec)
    out = _hy_final(bq5, inv1.astype(BF16), z4, x0c.reshape(b, n1 // 2, n2, c), bias)
    return out.reshape(b * l, c)


def _reorder_w_in(w):
    rest, gab, hy, gates = w[:, :4096], w[:, 4096:4112], w[:, 4112:5648], w[:, 5648:]
    pad = jnp.zeros((D_MODEL, D_PROJ - COL_GAB - 16), w.dtype)
    return jnp.concatenate([gates, rest, hy, gab, pad], axis=1).astype(BF16)


def _trunk(x3, p):
    b, l, _ = x3.shape
    t = b * l
    x = x3.reshape(t, D_MODEL)
    for i in range(DEPTH):
        proj = _proj(x, p['norm_mix'][i][None, :], p['w_in_r'][i])
        y_ret = retention(proj, b, l, COL_RET)
        gabh = jnp.transpose(proj[:, COL_GAB:COL_GAB + 4 * GDN_HEADS].reshape(t, 4, GDN_HEADS), (2, 0, 1))
        y_gdn = gdn(proj, gabh, p['gdn_conv_w'][i], p['gdn_A_log'][i], p['gdn_dt_bias'][i], p['gdn_onorm'][i],
                    b, l, COL_GDN)
        y_hy = hyena(proj, p['hy_conv_w'][i], p['hy_w1'][i], p['hy_b1'][i], p['hy_w2'][i], p['hy_b2'][i],
                     p['hy_w3'][i], p['hy_b3'][i], p['hy_freq'][i], p['hy_wout'][i], p['hy_decay'][i],
                     p['hy_bias'][i], b, l, COL_HY)
        x = _merge(x, proj, y_ret, y_gdn, y_hy, p['w_ret_o_b'][i], p['w_gdn_o_b'][i], p['w_hy_o_b'][i], p['w_o_b'][i])
        g = p['norm_ffn'][i][None, :]
        if i % 2 == 0:
            x = _ffn(x, g, p['dense_w_in_b'][i // 2], p['dense_w_out_b'][i // 2])
        else:
            x = _moe(x, g, p['moe_router_t'][i // 2], p['moe_w_in_b'][i // 2], p['moe_w_out_b'][i // 2])
    return _final_norm(x, p['norm_final'][None, :]).reshape(b, l, D_MODEL)


def kernel(x_prompt, x_sample, norm_mix, w_in, gdn_conv_w, gdn_A_log, gdn_dt_bias, gdn_onorm, hy_conv_w, hy_w1, hy_b1, hy_w2, hy_b2, hy_w3, hy_b3, hy_freq, hy_wout, hy_decay, hy_bias, w_ret_o, w_gdn_o, w_hy_o, w_o, norm_ffn, dense_w_in, dense_w_out, moe_router, moe_w_in, moe_w_out, norm_final):
    p = dict(norm_mix=norm_mix, gdn_conv_w=gdn_conv_w, gdn_A_log=gdn_A_log, gdn_dt_bias=gdn_dt_bias,
             gdn_onorm=gdn_onorm, hy_conv_w=hy_conv_w, hy_w1=hy_w1, hy_b1=hy_b1, hy_w2=hy_w2, hy_b2=hy_b2,
             hy_w3=hy_w3, hy_b3=hy_b3, hy_freq=hy_freq, hy_wout=hy_wout, hy_decay=hy_decay, hy_bias=hy_bias,
             norm_ffn=norm_ffn, norm_final=norm_final)
    p['w_in_r'] = jax.vmap(_reorder_w_in)(w_in)
    p['w_ret_o_b'] = w_ret_o.astype(BF16)
    p['w_gdn_o_b'] = w_gdn_o.astype(BF16)
    p['w_hy_o_b'] = w_hy_o.astype(BF16)
    p['w_o_b'] = w_o.astype(BF16)
    p['dense_w_in_b'] = dense_w_in.astype(BF16)
    p['dense_w_out_b'] = dense_w_out.astype(BF16)
    p['moe_router_t'] = jnp.swapaxes(moe_router, 1, 2)
    p['moe_w_in_b'] = moe_w_in.astype(BF16)
    p['moe_w_out_b'] = moe_w_out.astype(BF16)
    return (_trunk(x_prompt, p), _trunk(x_sample, p))
```

```python
import functools
import math

import jax
import jax.numpy as jnp
from jax import lax
from jax.experimental import pallas as pl
from jax.experimental.pallas import tpu as pltpu

F32 = jnp.float32
BF16 = jnp.bfloat16

D_MODEL = 1024
DEPTH = 2
RMS_EPS = 1e-6
RET_HEADS, RET_DK, RET_DV, RET_CHUNK = 4, 128, 128, 128
ROPE_BASE = 10000.0
GDN_HEADS, GDN_DK, GDN_DV, GDN_CHUNK, GDN_CONV = 4, 128, 128, 64, 5
HY_WIDTH, HY_SHORT, HY_EMB, HY_FFN = 512, 3, 33, 64
FF_DENSE = 2816
N_EXPERTS, TOP_K, FF_EXPERT = 8, 2, 3584

COL_GATES = 0
COL_RET = 3072
COL_GDN = 5120
COL_HY = 7168
COL_GAB = 8704
D_PROJ = 8832

VMEM_LIMIT = 56 * 1024 * 1024
HI = lax.Precision.HIGHEST


def _cparams(sem):
    return pltpu.CompilerParams(dimension_semantics=sem, vmem_limit_bytes=VMEM_LIMIT)


def _rms(x, g):
    return x * lax.rsqrt(jnp.mean(x * x, axis=-1, keepdims=True) + RMS_EPS) * g


def _sigmoid(x):
    return 1.0 / (1.0 + jnp.exp(-x))


def _bdot(a, b):
    return jnp.dot(a.astype(BF16), b, preferred_element_type=F32)


def _dot(a, b):
    return jnp.dot(a.astype(BF16), b.astype(BF16), preferred_element_type=F32)


def _dot_nt(a, b):
    return lax.dot_general(a.astype(BF16), b.astype(BF16), (((1,), (1,)), ((), ())), preferred_element_type=F32)


def _proj_body(x_ref, g_ref, w_ref, o_ref, u_ref):
    @pl.when(pl.program_id(1) == 0)
    def _():
        u_ref[...] = _rms(x_ref[...], g_ref[...]).astype(BF16)

    o_ref[...] = jnp.dot(u_ref[...], w_ref[...], preferred_element_type=F32)


def _proj(x, g, w, tm=1024, tn=2944):
    t = x.shape[0]
    return pl.pallas_call(
        _proj_body,
        grid=(t // tm, D_PROJ // tn),
        in_specs=[pl.BlockSpec((tm, D_MODEL), lambda i, j: (i, 0)),
                  pl.BlockSpec((1, D_MODEL), lambda i, j: (0, 0)),
                  pl.BlockSpec((D_MODEL, tn), lambda i, j: (0, j))],
        out_specs=pl.BlockSpec((tm, tn), lambda i, j: (i, j)),
        out_shape=jax.ShapeDtypeStruct((t, D_PROJ), F32),
        scratch_shapes=[pltpu.VMEM((tm, D_MODEL), BF16)],
        compiler_params=_cparams(("parallel", "arbitrary")),
        name="in_proj",
    )(x, g, w)


def _merge_body(x_ref, yr_ref, yg_ref, yh_ref, gr_ref, gg_ref, gh_ref, wr_ref, wg_ref, wh_ref, wo_ref, o_ref):
    merged = (_sigmoid(gr_ref[...]) * _bdot(yr_ref[...], wr_ref[...])
              + _sigmoid(gg_ref[...]) * _bdot(yg_ref[...], wg_ref[...])
              + _sigmoid(gh_ref[...]) * _bdot(yh_ref[...], wh_ref[...]))
    o_ref[...] = x_ref[...] + _bdot(merged, wo_ref[...])


def _merge(x, proj, y_ret, y_gdn, y_hy, w_ret, w_gdn, w_hy, w_o, tm=512):
    t = x.shape[0]
    row = lambda i: (i, 0)
    const = lambda i: (0, 0)
    gate = lambda b: pl.BlockSpec((tm, D_MODEL), lambda i: (i, COL_GATES // D_MODEL + b))
    return pl.pallas_call(
        _merge_body,
        grid=(t // tm,),
        in_specs=[pl.BlockSpec((tm, D_MODEL), row),
                  pl.BlockSpec((tm, 512), row), pl.BlockSpec((tm, 512), row), pl.BlockSpec((tm, 512), row),
                  gate(0), gate(1), gate(2),
                  pl.BlockSpec((512, D_MODEL), const), pl.BlockSpec((512, D_MODEL), const),
                  pl.BlockSpec((512, D_MODEL), const), pl.BlockSpec((D_MODEL, D_MODEL), const)],
        out_specs=pl.BlockSpec((tm, D_MODEL), row),
        out_shape=jax.ShapeDtypeStruct((t, D_MODEL), F32),
        compiler_params=_cparams(("parallel",)),
        name="merge",
    )(x, y_ret, y_gdn, y_hy, proj, proj, proj, w_ret, w_gdn, w_hy, w_o)


def _ffn_body(x_ref, g_ref, wa_ref, wb_ref, wo_ref, o_ref, u_ref, acc_ref):
    j = pl.program_id(1)

    @pl.when(j == 0)
    def _():
        u_ref[...] = _rms(x_ref[...], g_ref[...]).astype(BF16)
        acc_ref[...] = jnp.zeros_like(acc_ref)

    u = u_ref[...]
    a = jnp.dot(u, wa_ref[...], preferred_element_type=F32)
    b = jnp.dot(u, wb_ref[...], preferred_element_type=F32)
    acc_ref[...] += _bdot(a * _sigmoid(a) * b, wo_ref[...])

    @pl.when(j == pl.num_programs(1) - 1)
    def _():
        o_ref[...] = x_ref[...] + acc_ref[...]


def _ffn(x, g, w_in, w_out, tm=512, tf=1408):
    t = x.shape[0]
    nf = FF_DENSE // tf
    return pl.pallas_call(
        _ffn_body,
        grid=(t // tm, nf),
        in_specs=[pl.BlockSpec((tm, D_MODEL), lambda i, j: (i, 0)),
                  pl.BlockSpec((1, D_MODEL), lambda i, j: (0, 0)),
                  pl.BlockSpec((D_MODEL, tf), lambda i, j: (0, j)),
                  pl.BlockSpec((D_MODEL, tf), lambda i, j: (0, j + nf)),
                  pl.BlockSpec((tf, D_MODEL), lambda i, j: (j, 0))],
        out_specs=pl.BlockSpec((tm, D_MODEL), lambda i, j: (i, 0)),
        out_shape=jax.ShapeDtypeStruct((t, D_MODEL), F32),
        scratch_shapes=[pltpu.VMEM((tm, D_MODEL), BF16), pltpu.VMEM((tm, D_MODEL), F32)],
        compiler_params=_cparams(("parallel", "arbitrary")),
        name="dense_ffn",
    )(x, g, w_in, w_in, w_out)


def _route_body(x_ref, g_ref, wr_ref, u_ref, idx_ref, wt_ref):
    u = _rms(x_ref[...], g_ref[...])
    u_ref[...] = u.astype(BF16)
    logits = lax.dot_general(wr_ref[...], u, (((1,), (1,)), ((), ())), precision=HI, preferred_element_type=F32)
    row = lax.broadcasted_iota(jnp.int32, logits.shape, 0)
    m1 = jnp.max(logits, axis=0, keepdims=True)
    i1 = jnp.min(jnp.where(logits == m1, row, N_EXPERTS), axis=0, keepdims=True)
    rest = jnp.where(row == i1, -jnp.inf, logits)
    m2 = jnp.max(rest, axis=0, keepdims=True)
    i2 = jnp.min(jnp.where(rest == m2, row, N_EXPERTS), axis=0, keepdims=True)
    e = jnp.exp(m2 - m1)
    idx_ref[0:1, :] = i1
    idx_ref[1:2, :] = i2
    wt_ref[0:1, :] = 1.0 / (1.0 + e)
    wt_ref[1:2, :] = e / (1.0 + e)


def _route(x, g, wr_t, tm=512):
    t = x.shape[0]
    return pl.pallas_call(
        _route_body,
        grid=(t // tm,),
        in_specs=[pl.BlockSpec((tm, D_MODEL), lambda i: (i, 0)),
                  pl.BlockSpec((1, D_MODEL), lambda i: (0, 0)),
                  pl.BlockSpec((N_EXPERTS, D_MODEL), lambda i: (0, 0))],
        out_specs=[pl.BlockSpec((tm, D_MODEL), lambda i: (i, 0)),
                   pl.BlockSpec((TOP_K, tm), lambda i: (0, i)),
                   pl.BlockSpec((TOP_K, tm), lambda i: (0, i))],
        out_shape=[jax.ShapeDtypeStruct((t, D_MODEL), BF16),
                   jax.ShapeDtypeStruct((TOP_K, t), jnp.int32),
                   jax.ShapeDtypeStruct((TOP_K, t), F32)],
        compiler_params=_cparams(("parallel",)),
        name="moe_router",
    )(x, g, wr_t)


def _experts_body(te_ref, tv_ref, x_ref, sw_ref, wa_ref, wb_ref, wo_ref, o_ref, acc_ref):
    i, j = pl.program_id(0), pl.program_id(1)

    @pl.when(j == 0)
    def _():
        acc_ref[...] = jnp.zeros_like(acc_ref)

    @pl.when(tv_ref[i] != 0)
    def _():
        x = x_ref[...]
        a = jnp.dot(x, wa_ref[...], preferred_element_type=F32)
        b = jnp.dot(x, wb_ref[...], preferred_element_type=F32)
        acc_ref[...] += _bdot(a * _sigmoid(a) * b, wo_ref[...])

    @pl.when(j == pl.num_programs(1) - 1)
    def _():
        o_ref[...] = acc_ref[...] * sw_ref[...]


def _experts(tile_expert, tile_valid, xs, sw, w_in, w_out, tm, tf=512):
    p = xs.shape[0]
    nf = FF_EXPERT // tf
    grid_spec = pltpu.PrefetchScalarGridSpec(
        num_scalar_prefetch=2,
        grid=(p // tm, nf),
        in_specs=[pl.BlockSpec((tm, D_MODEL), lambda i, j, te, tv: (i, 0)),
                  pl.BlockSpec((tm, 1), lambda i, j, te, tv: (i, 0)),
                  pl.BlockSpec((None, D_MODEL, tf), lambda i, j, te, tv: (te[i], 0, j)),
                  pl.BlockSpec((None, D_MODEL, tf), lambda i, j, te, tv: (te[i], 0, j + nf)),
                  pl.BlockSpec((None, tf, D_MODEL), lambda i, j, te, tv: (te[i], j, 0))],
        out_specs=pl.BlockSpec((tm, D_MODEL), lambda i, j, te, tv: (i, 0)),
        scratch_shapes=[pltpu.VMEM((tm, D_MODEL), F32)],
    )
    return pl.pallas_call(
        _experts_body,
        grid_spec=grid_spec,
        out_shape=jax.ShapeDtypeStruct((p, D_MODEL), F32),
        compiler_params=_cparams(("parallel", "arbitrary")),
        name="moe_experts",
    )(tile_expert, tile_valid, xs, sw, w_in, w_in, w_out)


def _moe(x, g, wr_t, w_in, w_out, tm=1024):
    t = x.shape[0]
    u, idx, wts = _route(x, g, wr_t)
    e_flat = idx.reshape(-1)
    onehot = (e_flat[:, None] == jnp.arange(N_EXPERTS, dtype=jnp.int32)[None, :]).astype(jnp.int32)
    csum = jnp.cumsum(onehot, axis=0)
    rank = jnp.sum((csum - onehot) * onehot, axis=1)
    counts = csum[-1]
    padded = ((counts + tm - 1) // tm) * tm
    ends = jnp.cumsum(padded)
    starts = ends - padded
    dest = starts[e_flat] + rank
    p = TOP_K * t + N_EXPERTS * tm
    tok = jnp.tile(jnp.arange(t, dtype=jnp.int32), TOP_K)
    pair = jnp.stack([tok, lax.bitcast_convert_type(wts.reshape(-1), jnp.int32)], axis=1)
    sorted_pair = jnp.zeros((p, 2), jnp.int32).at[dest].set(pair)
    sorted_tok = sorted_pair[:, 0]
    sorted_w = lax.bitcast_convert_type(sorted_pair[:, 1], F32)
    tile_start = jnp.arange(p // tm, dtype=jnp.int32) * tm
    tile_valid = (tile_start < ends[-1]).astype(jnp.int32)
    tile_expert = jnp.minimum(jnp.sum((tile_start[:, None] >= ends[None, :]).astype(jnp.int32), axis=1),
                              N_EXPERTS - 1)
    last_e = jnp.max(jnp.where(counts > 0, jnp.arange(N_EXPERTS, dtype=jnp.int32), 0))
    tile_expert = jnp.where(tile_valid != 0, tile_expert, last_e)
    xs = jnp.take(u, sorted_tok, axis=0)
    ys = _experts(tile_expert, tile_valid, xs, sorted_w[:, None], w_in, w_out, tm)
    dest2 = dest.reshape(TOP_K, t)
    return x + jnp.take(ys, dest2[0], axis=0) + jnp.take(ys, dest2[1], axis=0)


def _norm_body(x_ref, g_ref, o_ref):
    o_ref[...] = _rms(x_ref[...], g_ref[...])


def _final_norm(x, g, tm=1024):
    t = x.shape[0]
    return pl.pallas_call(
        _norm_body,
        grid=(t // tm,),
        in_specs=[pl.BlockSpec((tm, D_MODEL), lambda i: (i, 0)), pl.BlockSpec((1, D_MODEL), lambda i: (0, 0))],
        out_specs=pl.BlockSpec((tm, D_MODEL), lambda i: (i, 0)),
        out_shape=jax.ShapeDtypeStruct((t, D_MODEL), F32),
        compiler_params=_cparams(("parallel",)),
        name="final_norm",
    )(x, g)


RET_GROUP = 4


def _ret_tables(l):
    half = RET_DK // 2
    inv_freq = ROPE_BASE ** (-jnp.arange(half, dtype=F32) / half)
    ang = jnp.arange(l, dtype=F32)[:, None] * inv_freq[None, :]
    cos, sin = jnp.cos(ang), jnp.sin(ang)
    cosf = jnp.concatenate([cos, cos], axis=1)
    sinf = jnp.concatenate([-sin, sin], axis=1)
    c = RET_CHUNK
    log_gamma = jnp.log(1.0 - jnp.exp2(-5.0 - jnp.arange(RET_HEADS, dtype=F32)))[:, None, None]
    idx = jnp.arange(c, dtype=F32)
    dmat = jnp.exp(jnp.abs(idx[:, None] - idx[None, :])[None] * log_gamma)
    col = lambda e: jnp.broadcast_to(jnp.exp(e[None, :, None] * log_gamma), (RET_HEADS, c, 128))
    sc = jnp.stack([col(idx + 1.0), col(c - idx), col(c - 1.0 - idx), col(idx),
                    jnp.broadcast_to(jnp.exp(c * log_gamma), (RET_HEADS, c, 128))], axis=1)
    return cosf, sinf, dmat, sc


def _ret_body(q_ref, k_ref, v_ref, g_ref, cos_ref, sin_ref, dm_ref, sc_ref, o_ref, qs, qd, ks, vs, uf, ub):
    c = RET_CHUNK
    n_chunks = q_ref.shape[0] // c
    q_in, q_out, k_in, k_out = sc_ref[0], sc_ref[1], sc_ref[2], sc_ref[3]
    chunk_decay = sc_ref[4]

    def rot(x, cos, sin):
        return x * cos + pltpu.roll(x, RET_DK // 2, 1) * sin

    grp = RET_GROUP

    def prep(i, carry):
        ks_f, ks_b, vs_ = [], [], []
        for j in range(grp):
            rows = pl.ds(pl.multiple_of((i * grp + j) * c, c), c)
            cos, sin = cos_ref[rows, :], sin_ref[rows, :]
            q = rot(q_ref[rows, :], cos, sin)
            k = rot(k_ref[rows, :], cos, sin) * (RET_DK ** -0.5)
            v = v_ref[rows, :].astype(BF16)
            qs[rows, :] = q.astype(BF16)
            qd[rows, :] = jnp.concatenate([q * q_in, q * q_out], axis=1).astype(BF16)
            ks[rows, :] = k.astype(BF16)
            vs[rows, :] = v
            ks_f.append((k * k_in).T.astype(BF16))
            ks_b.append((k * k_out).T.astype(BF16))
            vs_.append(v)
        for j in range(grp):
            uf[i * grp + j] = jnp.dot(ks_f[j], vs_[j], preferred_element_type=F32)
            ub[i * grp + j] = jnp.dot(ks_b[j], vs_[j], preferred_element_type=F32)
        return carry

    lax.fori_loop(0, n_chunks // grp, prep, 0)

    def fwd_state(n, s):
        u = uf[n]
        uf[n] = s
        return s * chunk_decay + u

    lax.fori_loop(0, n_chunks, fwd_state, jnp.zeros((RET_DK, RET_DV), F32))

    def bwd_state(i, s):
        n = n_chunks - 1 - i
        u = ub[n]
        ub[n] = s
        return s * chunk_decay + u

    lax.fori_loop(0, n_chunks, bwd_state, jnp.zeros((RET_DK, RET_DV), F32))

    def out(i, carry):
        rows = [pl.ds(pl.multiple_of((i * grp + j) * c, c), c) for j in range(grp)]
        ss = [_dot_nt(qs[r, :], ks[r, :]) for r in rows]
        ss = [(s * dm_ref[...]).astype(BF16) for s in ss]
        states = [jnp.concatenate([uf[i * grp + j], ub[i * grp + j]], axis=0).astype(BF16) for j in range(grp)]
        os_ = [jnp.dot(s, vs[r, :], preferred_element_type=F32) for s, r in zip(ss, rows)]
        os_ = [o + jnp.dot(qd[r, :], st, preferred_element_type=F32) for o, r, st in zip(os_, rows, states)]
        for o, r in zip(os_, rows):
            mu = jnp.mean(o, axis=-1, keepdims=True)
            d = o - mu
            o = d * lax.rsqrt(jnp.mean(d * d, axis=-1, keepdims=True) + RMS_EPS)
            g = g_ref[r, :]
            o_ref[r, :] = g * _sigmoid(g) * o
        return carry

    lax.fori_loop(0, n_chunks // grp, out, 0)


def retention(proj, b, l, col0):
    cosf, sinf, dmat, sc = _ret_tables(l)
    cb = col0 // 128
    nh = RET_HEADS
    c = RET_CHUNK
    seq = lambda off: pl.BlockSpec((l, 128), lambda i, h: (i, cb + off * nh + h))
    const2 = pl.BlockSpec((l, 128), lambda i, h: (0, 0))
    return pl.pallas_call(
        _ret_body,
        grid=(b, nh),
        in_specs=[seq(0), seq(1), seq(2), seq(3), const2, const2,
                  pl.BlockSpec((None, c, c), lambda i, h: (h, 0, 0)),
                  pl.BlockSpec((None, 5, c, 128), lambda i, h: (h, 0, 0, 0))],
        out_specs=pl.BlockSpec((l, 128), lambda i, h: (i, h)),
        out_shape=jax.ShapeDtypeStruct((b * l, nh * RET_DV), F32),
        scratch_shapes=[pltpu.VMEM((l, 128), BF16), pltpu.VMEM((l, 256), BF16), pltpu.VMEM((l, 128), BF16),
                        pltpu.VMEM((l, 128), BF16), pltpu.VMEM((l // c, 128, 128), F32),
                        pltpu.VMEM((l // c, 128, 128), F32)],
        compiler_params=_cparams(("parallel", "parallel")),
        name="retention",
    )(proj, proj, proj, proj, cosf, sinf, dmat, sc)


GDN_ROWS = 512
HALO = 8


def _cumsum_rows(x, reverse):
    n = x.shape[0]
    row = lax.broadcasted_iota(jnp.int32, x.shape, 0)
    s = 1
    while s < n:
        if reverse:
            x = x + jnp.where(row < n - s, pltpu.roll(x, n - s, 0), 0.0)
        else:
            x = x + jnp.where(row >= s, pltpu.roll(x, s, 0), 0.0)
        s *= 2
    return x


def _gdn_local(insts, neg_a, dtb):
    c = GDN_CHUNK
    row = lax.broadcasted_iota(jnp.int32, (c, c), 0)
    col = lax.broadcasted_iota(jnp.int32, (c, c), 1)
    pre = []
    for q, k, v, graw, braw, d in insts:
        reverse = d == 1
        x = graw + dtb[d]
        softplus = jnp.maximum(x, 0.0) + jnp.log(1.0 + jnp.exp(-jnp.abs(x)))
        log_g = neg_a[d] * softplus
        beta = _sigmoid(braw)
        cum = _cumsum_rows(jnp.broadcast_to(log_g, (c, 128)), reverse)
        total = cum[0:1, :] if reverse else cum[c - 1:c, :]
        cum_sq = cum[:, :c]
        cum_row = jnp.sum(jnp.where(row == col, cum_sq, 0.0), axis=0, keepdims=True)
        incl = (row <= col) if reverse else (row >= col)
        strict = (row < col) if reverse else (row > col)
        decay = jnp.where(incl, jnp.exp(jnp.where(incl, cum_sq - cum_row, 0.0)), 0.0)
        pre.append((cum, total, decay, strict, k * beta, v * beta))
    qks = [_dot_nt(jnp.concatenate([q, p[4]], axis=0), k) for (q, k, *_), p in zip(insts, pre)]
    bps = [-jnp.where(p[3], qk[c:] * p[2], 0.0) for qk, p in zip(qks, pre)]
    rs = list(bps)
    for _ in range(5):
        bps = [_dot(bp, bp) for bp in bps]
        rs = [r + bp + _dot(r, bp) for r, bp in zip(rs, bps)]
    rhss = [jnp.concatenate([p[5], p[4] * jnp.exp(p[0])], axis=1) for p in pre]
    sols = [rhs + _dot(r, rhs) for r, rhs in zip(rs, rhss)]
    kdts, attns = [], []
    for (q, k, *_), p, qk in zip(insts, pre, qks):
        cum, total, decay = p[0], p[1], p[2]
        k_dec = k * jnp.exp(total - cum)
        kdts.append(jnp.concatenate([k_dec, jnp.zeros((c, 128), F32)], axis=0).T[:, :c])
        attns.append(qk[:c] * decay)
    kus = [_dot(kd_t, sol) for kd_t, sol in zip(kdts, sols)]
    aos = [_dot(attn, sol) for attn, sol in zip(attns, sols)]
    outs = []
    for (q, *_), p, ku, ao in zip(insts, pre, kus, aos):
        q_eff = q * jnp.exp(p[0]) - ao[:, GDN_DV:]
        outs.append((jnp.concatenate([ku[:, GDN_DV:], q_eff], axis=0), ku[:, :GDN_DV], ao[:, :GDN_DV], jnp.exp(p[1])))
    return outs


def _gdn_body(alog_ref, dtb_ref, q_ref, k_ref, v_ref, gate_ref, gab_ref, cwq_ref, cwk_ref, cwv_ref, onorm_ref,
              o_ref, pad, km_s, u_s, o0_s, gl_s, sf, sb):
    c = GDN_CHUNK
    l = q_ref.shape[0]
    h = pl.program_id(1)
    n_chunks = l // c
    per_block = GDN_ROWS // c

    zero_halo = jnp.zeros((HALO, 128), F32)
    for a, src in enumerate((q_ref, k_ref, v_ref)):
        pad[a, 0:HALO, :] = zero_halo
        pad[a, l + HALO:l + 2 * HALO, :] = zero_halo
        pad[a, HALO:l + HALO, :] = src[...]

    ones = jnp.ones((c, 1), F32)
    neg_a = (-jnp.exp(ones * alog_ref[0, h]), -jnp.exp(ones * alog_ref[1, h]))
    dtb = (dtb_ref[0, h], dtb_ref[1, h])
    win_rows = GDN_ROWS + 2 * HALO

    def local_block(i, carry):
        s0 = pl.multiple_of(i * GDN_ROWS, GDN_ROWS)
        ys = []
        for a, cw_ref in enumerate((cwq_ref, cwk_ref, cwv_ref)):
            win = pad[a, pl.ds(s0, win_rows), :]
            acc = cw_ref[2:3, :] * win
            for j in (0, 1, 3, 4):
                acc = acc + cw_ref[j:j + 1, :] * pltpu.roll(win, (GDN_CONV // 2 - j) % win_rows, 0)
            y = acc[HALO:HALO + GDN_ROWS]
            y = y * _sigmoid(y)
            if a < 2:
                y = y * lax.rsqrt(jnp.sum(y * y, axis=-1, keepdims=True) + 1e-6)
            if a == 0:
                y = y * (GDN_DK ** -0.5)
            ys.append(y)
        ab = gab_ref[pl.ds(s0, GDN_ROWS), :]
        insts = []
        for j in range(per_block):
            sl = slice(j * c, (j + 1) * c)
            for d in range(2):
                insts.append((ys[0][sl], ys[1][sl], ys[2][sl], ab[sl, d:d + 1], ab[sl, 2 + d:3 + d], d))
        outs = _gdn_local(insts, neg_a, dtb)
        for idx, (km, u, o0, g_last) in enumerate(outs):
            m = (idx % 2) * n_chunks + i * per_block + idx // 2
            km_s[m] = km.astype(BF16)
            u_s[m] = u.astype(BF16)
            o0_s[m] = o0
            gl_s[m] = jnp.broadcast_to(g_last, (8, 128))
        return carry

    lax.fori_loop(0, l // GDN_ROWS, local_block, 0)

    sf[...] = jnp.zeros_like(sf)
    sb[...] = jnp.zeros_like(sb)

    def scan(i, carry):
        for d, s_ref in ((0, sf), (1, sb)):
            n = i if d == 0 else n_chunks - 1 - i
            m = d * n_chunks + n
            rows = pl.ds(pl.multiple_of(n * c, c) + HALO, c)
            s = s_ref[...]
            r = jnp.dot(km_s[m], s.astype(BF16), preferred_element_type=F32)
            pad[d, rows, :] = o0_s[m] + r[GDN_DK:]
            s_ref[...] = s * gl_s[m][0:1, :] - r[:GDN_DK] + u_s[m].astype(F32)
        return carry

    lax.fori_loop(0, n_chunks, scan, 0)

    def fin(i, carry):
        r0 = pl.multiple_of(i * GDN_ROWS, GDN_ROWS)
        o = pad[0, pl.ds(r0 + HALO, GDN_ROWS), :] + pad[1, pl.ds(r0 + HALO, GDN_ROWS), :]
        o = o * lax.rsqrt(jnp.mean(o * o, axis=-1, keepdims=True) + RMS_EPS) * onorm_ref[...]
        g = gate_ref[pl.ds(r0, GDN_ROWS), :]
        o_ref[pl.ds(r0, GDN_ROWS), :] = o * (g * _sigmoid(g))
        return carry

    lax.fori_loop(0, l // GDN_ROWS, fin, 0)


def gdn(proj, gabh, conv_w, a_log, dt_bias, onorm, b, l, col0):
    cb = col0 // 128
    nh = GDN_HEADS
    nc = l // GDN_CHUNK
    seq = lambda off: pl.BlockSpec((l, 128), lambda i, h, *_: (i, cb + off * nh + h))
    cw = lambda off: pl.BlockSpec((GDN_CONV, 128), lambda i, h, *_: (0, off * nh + h))
    grid_spec = pltpu.PrefetchScalarGridSpec(
        num_scalar_prefetch=2,
        grid=(b, nh),
        in_specs=[seq(0), seq(1), seq(2), seq(3),
                  pl.BlockSpec((None, l, 4), lambda i, h, *_: (h, i, 0)),
                  cw(0), cw(1), cw(2),
                  pl.BlockSpec((1, 128), lambda i, h, *_: (0, 0))],
        out_specs=pl.BlockSpec((l, 128), lambda i, h, *_: (i, h)),
        scratch_shapes=[pltpu.VMEM((3, l + 2 * HALO, 128), F32),
                        pltpu.VMEM((2 * nc, GDN_DK + GDN_CHUNK, GDN_DK), BF16),
                        pltpu.VMEM((2 * nc, GDN_DK, GDN_DV), BF16),
                        pltpu.VMEM((2 * nc, GDN_CHUNK, GDN_DV), F32),
                        pltpu.VMEM((2 * nc, 8, 128), F32),
                        pltpu.VMEM((GDN_DK, GDN_DV), F32), pltpu.VMEM((GDN_DK, GDN_DV), F32)],
    )
    return pl.pallas_call(
        _gdn_body,
        grid_spec=grid_spec,
        out_shape=jax.ShapeDtypeStruct((b * l, nh * GDN_DV), F32),
        compiler_params=_cparams(("parallel", "parallel")),
        name="gated_deltanet",
    )(a_log, dt_bias, proj, proj, proj, proj, gabh, conv_w, conv_w, conv_w, onorm[None, :])


HY_N1 = 64
HY_ROWS = 256


def _hyconv_body(x0_ref, x1_ref, v_ref, c0_ref, c1_ref, cv_ref, z_ref, x0c_ref, pad):
    l = x0_ref.shape[0]
    zero_halo = jnp.zeros((HALO, 128), F32)
    for a, src in enumerate((x0_ref, x1_ref, v_ref)):
        pad[a, 0:HALO, :] = zero_halo
        pad[a, l + HALO:l + 2 * HALO, :] = zero_halo
        pad[a, HALO:l + HALO, :] = src[...]
    win_rows = HY_ROWS + 2 * HALO

    def block(i, carry):
        s0 = pl.multiple_of(i * HY_ROWS, HY_ROWS)
        outs = []
        for a, cw_ref in enumerate((c0_ref, c1_ref, cv_ref)):
            win = pad[a, pl.ds(s0, win_rows), :]
            acc = cw_ref[1:2, :] * win
            acc = acc + cw_ref[0:1, :] * pltpu.roll(win, 1, 0)
            acc = acc + cw_ref[2:3, :] * pltpu.roll(win, win_rows - 1, 0)
            outs.append(acc[HALO:HALO + HY_ROWS])
        rows = pl.ds(s0, HY_ROWS)
        x0c_ref[rows, :] = outs[0]
        z_ref[rows, :] = outs[2] * outs[1]
        return carry

    lax.fori_loop(0, l // HY_ROWS, block, 0)


def _hy_conv(proj, conv_w, b, l, col0):
    cb = col0 // 128
    nc = HY_WIDTH // 128
    seq = lambda off: pl.BlockSpec((l, 128), lambda i, c: (i, cb + off * nc + c))
    cw = lambda off: pl.BlockSpec((HY_SHORT, 128), lambda i, c: (0, off * nc + c))
    out = pl.BlockSpec((l, 128), lambda i, c: (i, c))
    return pl.pallas_call(
        _hyconv_body,
        grid=(b, nc),
        in_specs=[seq(0), seq(1), seq(2), cw(0), cw(1), cw(2)],
        out_specs=[out, out],
        out_shape=[jax.ShapeDtypeStruct((b * l, HY_WIDTH), F32), jax.ShapeDtypeStruct((b * l, HY_WIDTH), F32)],
        scratch_shapes=[pltpu.VMEM((3, l + 2 * HALO, 128), F32)],
        compiler_params=_cparams(("parallel", "parallel")),
        name="hyena_short_conv",
    )(proj, proj, proj, conv_w, conv_w, conv_w)


def _hyfilt_body(feat_ref, w1_ref, b1_ref, w2_ref, b2_ref, w3_ref, b3_ref, fr_ref, wo_ref, dec_ref, o_ref, *, l):
    tm = feat_ref.shape[0]
    hdot = lambda a, b_: jnp.dot(a, b_, precision=HI, preferred_element_type=F32)
    feat = feat_ref[...]
    fr = fr_ref[...]
    h = jnp.sin(fr * (hdot(feat, w1_ref[...]) + b1_ref[...]))
    h = jnp.sin(fr * (hdot(h, w2_ref[...]) + b2_ref[...]))
    h = jnp.sin(fr * (hdot(h, w3_ref[...]) + b3_ref[...]))
    h = hdot(h, wo_ref[...]) * jnp.exp(-feat[:, 0:1] * jnp.abs(dec_ref[...]))
    m = pl.program_id(0) * tm + lax.broadcasted_iota(jnp.int32, (tm, 1), 0)
    o_ref[...] = jnp.where(m == l, 0.0, h)


def _hy_filter(l, w1, b1, w2, b2, w3, b3, freq, wout, decay, tm=512):
    n = 2 * l
    m = jnp.arange(n, dtype=jnp.int32)
    t = jnp.where(m < l, m, n - m).astype(F32)[:, None]
    t01 = t / (l - 1)
    bands = (HY_EMB - 1) // 2
    f = jnp.linspace(1e-4, bands - 1, bands, dtype=F32)[None, :]
    w = 2.0 * math.pi * t / l
    feat = jnp.concatenate([t01, jnp.cos(f * w), -jnp.sin(f * w)], axis=-1)
    const = lambda i: (0, 0)
    half = lambda i: (0, (i * tm) // l)
    vec = lambda x: x[None, :]
    return pl.pallas_call(
        functools.partial(_hyfilt_body, l=l),
        grid=(n // tm,),
        in_specs=[pl.BlockSpec((tm, HY_EMB), lambda i: (i, 0)),
                  pl.BlockSpec((HY_EMB, HY_FFN), const), pl.BlockSpec((1, HY_FFN), const),
                  pl.BlockSpec((HY_FFN, HY_FFN), const), pl.BlockSpec((1, HY_FFN), const),
                  pl.BlockSpec((HY_FFN, HY_FFN), const), pl.BlockSpec((1, HY_FFN), const),
                  pl.BlockSpec((1, HY_FFN), const),
                  pl.BlockSpec((HY_FFN, HY_WIDTH), half),
                  pl.BlockSpec((None, 1, HY_WIDTH), lambda i: ((i * tm) // l, 0, 0))],
        out_specs=pl.BlockSpec((tm, HY_WIDTH), lambda i: (i, 0)),
        out_shape=jax.ShapeDtypeStruct((n, HY_WIDTH), F32),
        compiler_params=_cparams(("parallel",)),
        name="hyena_filter",
    )(feat, w1, vec(b1), w2, vec(b2), w3, vec(b3), vec(freq), wout, decay[:, None, :])


def _dft_tables(l):
    n = 2 * l
    n1, n2 = HY_N1, n // HY_N1
    ang = lambda num, den: (2.0 * math.pi / den) * (num % den).astype(F32)
    i1 = jnp.arange(n1, dtype=jnp.int32)
    a1 = ang(i1[:, None] * i1[None, :], n1)
    fwd1 = jnp.concatenate([jnp.cos(a1), -jnp.sin(a1)], axis=0)
    a1h = a1[:, :n1 // 2].T
    inv1 = jnp.concatenate([jnp.cos(a1h), -jnp.sin(a1h)], axis=1) / n
    i2 = jnp.arange(n2, dtype=jnp.int32)
    freq = i1[:, None, None] + n1 * i2[None, :, None]
    a2 = ang(freq * i2[None, None, :], n)
    gr, gi = jnp.cos(a2), -jnp.sin(a2)
    fwd2 = jnp.concatenate([jnp.concatenate([gr, -gi], axis=2), jnp.concatenate([gi, gr], axis=2)], axis=1)
    inv2 = jnp.swapaxes(fwd2, 1, 2)
    return fwd1, inv1, fwd2, inv2


HY_TR = 8


def _dft1_body(x_ref, f_ref, o_ref, *, exact):
    n1 = o_ref.shape[1]
    for r in range(x_ref.shape[1]):
        x = x_ref[:, r, :]
        if exact:
            o = jnp.dot(f_ref[...], x, precision=HI, preferred_element_type=F32)
        else:
            o = jnp.dot(f_ref[...], x.astype(BF16), preferred_element_type=F32)
        o_ref[0, :, r, :] = o[:n1].astype(o_ref.dtype)
        o_ref[1, :, r, :] = o[n1:].astype(o_ref.dtype)


def _dft1(x4, mat, out_dtype, exact):
    bt, k1, n2, c = x4.shape
    n1 = mat.shape[0] // 2
    return pl.pallas_call(
        functools.partial(_dft1_body, exact=exact),
        grid=(bt, n2 // HY_TR),
        in_specs=[pl.BlockSpec((None, k1, HY_TR, c), lambda i, j: (i, 0, j, 0)),
                  pl.BlockSpec((2 * n1, k1), lambda i, j: (0, 0))],
        out_specs=pl.BlockSpec((None, 2, n1, HY_TR, c), lambda i, j: (i, 0, 0, j, 0)),
        out_shape=jax.ShapeDtypeStruct((bt, 2, n1, n2, c), out_dtype),
        compiler_params=_cparams(("parallel", "parallel")),
        name="hyena_dft_stage1",
    )(x4, mat)


def _spec2_body(a_ref, g_ref, o_ref):
    n2 = a_ref.shape[1]
    a = a_ref[...].reshape(2 * n2, a_ref.shape[2])
    o = jnp.dot(g_ref[...], a, precision=HI, preferred_element_type=F32)
    o_ref[...] = o.reshape(o_ref.shape)


def _filter_spectrum(circ, fwd1, fwd2, l):
    n1, n2 = HY_N1, 2 * l // HY_N1
    a5 = _dft1(circ.reshape(1, n1, n2, HY_WIDTH), fwd1, F32, True)[0]
    return pl.pallas_call(
        _spec2_body,
        grid=(n1,),
        in_specs=[pl.BlockSpec((2, None, n2, HY_WIDTH), lambda i: (0, i, 0, 0)),
                  pl.BlockSpec((None, 2 * n2, 2 * n2), lambda i: (i, 0, 0))],
        out_specs=pl.BlockSpec((None, 2, n2, HY_WIDTH), lambda i: (i, 0, 0, 0)),
        out_shape=jax.ShapeDtypeStruct((n1, 2, n2, HY_WIDTH), F32),
        compiler_params=_cparams(("parallel",)),
        name="hyena_filter_spectrum",
    )(a5, fwd2)


def _hymid_body(a_ref, g_ref, gi_ref, h_ref, o_ref):
    bb, _, n2, c = a_ref.shape
    a = jnp.concatenate([a_ref[i].reshape(2 * n2, c) for i in range(bb)], axis=1)
    x = jnp.dot(g_ref[...], a, preferred_element_type=F32)
    hr = jnp.concatenate([h_ref[0]] * bb, axis=1)
    hi = jnp.concatenate([h_ref[1]] * bb, axis=1)
    xr, xi = x[:n2], x[n2:]
    y = jnp.concatenate([xr * hr - xi * hi, xr * hi + xi * hr], axis=0).astype(BF16)
    o = jnp.dot(gi_ref[...], y, preferred_element_type=F32).astype(o_ref.dtype)
    for i in range(bb):
        o_ref[i] = o[:, i * c:(i + 1) * c].reshape(2, n2, c)


def _hy_mid(a5, fwd2, inv2, spec, bb=4):
    b, _, n1, n2, c = a5.shape
    blk = pl.BlockSpec((bb, 2, None, n2, c), lambda f, i: (i, 0, f, 0, 0))
    mat = pl.BlockSpec((None, 2 * n2, 2 * n2), lambda f, i: (f, 0, 0))
    return pl.pallas_call(
        _hymid_body,
        grid=(n1, b // bb),
        in_specs=[blk, mat, mat, pl.BlockSpec((None, 2, n2, c), lambda f, i: (f, 0, 0, 0))],
        out_specs=blk,
        out_shape=jax.ShapeDtypeStruct(a5.shape, BF16),
        compiler_params=_cparams(("parallel", "parallel")),
        name="hyena_spectral_product",
    )(a5, fwd2, inv2, spec)


def _hyfin_body(b_ref, m_ref, z_ref, x0_ref, bias_ref, o_ref):
    for r in range(z_ref.shape[1]):
        rhs = jnp.concatenate([b_ref[0, :, r, :], b_ref[1, :, r, :]], axis=0)
        y = jnp.dot(m_ref[...], rhs, preferred_element_type=F32)
        o_ref[:, r, :] = (y + z_ref[:, r, :] * bias_ref[...]) * x0_ref[:, r, :]


def _hy_final(bq5, inv1, z4, x04, bias):
    b, _, n1, n2, c = bq5.shape
    k1 = z4.shape[1]
    row = pl.BlockSpec((None, k1, HY_TR, c), lambda i, j: (i, 0, j, 0))
    return pl.pallas_call(
        _hyfin_body,
        grid=(b, n2 // HY_TR),
        in_specs=[pl.BlockSpec((None, 2, n1, HY_TR, c), lambda i, j: (i, 0, 0, j, 0)),
                  pl.BlockSpec((k1, 2 * n1), lambda i, j: (0, 0)), row, row,
                  pl.BlockSpec((1, c), lambda i, j: (0, 0))],
        out_specs=row,
        out_shape=jax.ShapeDtypeStruct(z4.shape, F32),
        compiler_params=_cparams(("parallel", "parallel")),
        name="hyena_dft_final",
    )(bq5, inv1, z4, x04, bias[None, :])


def hyena(proj, conv_w, w1, b1, w2, b2, w3, b3, freq, wout, decay, bias, b, l, col0):
    n1, n2, c = HY_N1, 2 * l // HY_N1, HY_WIDTH
    fwd1, inv1, fwd2, inv2 = _dft_tables(l)
    circ = _hy_filter(l, w1, b1, w2, b2, w3, b3, freq, wout, decay)
    spec = _filter_spectrum(circ, fwd1, fwd2, l)
    z, x0c = _hy_conv(proj, conv_w, b, l, col0)
    z4 = z.reshape(b, n1 // 2, n2, c)
    a5 = _dft1(z4, fwd1[:, :n1 // 2].astype(BF16), BF16, False)
    bq5 = _hy_mid(a5, fwd2.astype(BF16), inv2.astype(BF16), spec)
    out = _hy_final(bq5, inv1.astype(BF16), z4, x0c.reshape(b, n1 // 2, n2, c), bias)
    return out.reshape(b * l, c)


def _reorder_w_in(w):
    rest, gab, hy, gates = w[:, :4096], w[:, 4096:4112], w[:, 4112:5648], w[:, 5648:]
    pad = jnp.zeros((D_MODEL, D_PROJ - COL_GAB - 16), w.dtype)
    return jnp.concatenate([gates, rest, hy, gab, pad], axis=1).astype(BF16)


def _trunk(x3, p):
    b, l, _ = x3.shape
    t = b * l
    x = x3.reshape(t, D_MODEL)
    for i in range(DEPTH):
        proj = _proj(x, p['norm_mix'][i][None, :], p['w_in_r'][i])
        y_ret = retention(proj, b, l, COL_RET)
        gabh = jnp.transpose(proj[:, COL_GAB:COL_GAB + 4 * GDN_HEADS].reshape(t, 4, GDN_HEADS), (2, 0, 1))
        y_gdn = gdn(proj, gabh, p['gdn_conv_w'][i], p['gdn_A_log'][i], p['gdn_dt_bias'][i], p['gdn_onorm'][i],
                    b, l, COL_GDN)
        y_hy = hyena(proj, p['hy_conv_w'][i], p['hy_w1'][i], p['hy_b1'][i], p['hy_w2'][i], p['hy_b2'][i],
                     p['hy_w3'][i], p['hy_b3'][i], p['hy_freq'][i], p['hy_wout'][i], p['hy_decay'][i],
                     p['hy_bias'][i], b, l, COL_HY)
        x = _merge(x, proj, y_ret, y_gdn, y_hy, p['w_ret_o_b'][i], p['w_gdn_o_b'][i], p['w_hy_o_b'][i], p['w_o_b'][i])
        g = p['norm_ffn'][i][None, :]
        if i % 2 == 0:
            x = _ffn(x, g, p['dense_w_in_b'][i // 2], p['dense_w_out_b'][i // 2])
        else:
            x = _moe(x, g, p['moe_router_t'][i // 2], p['moe_w_in_b'][i // 2], p['moe_w_out_b'][i // 2])
    return _final_norm(x, p['norm_final'][None, :]).reshape(b, l, D_MODEL)


def kernel(x_prompt, x_sample, norm_mix, w_in, gdn_conv_w, gdn_A_log, gdn_dt_bias, gdn_onorm, hy_conv_w, hy_w1, hy_b1, hy_w2, hy_b2, hy_w3, hy_b3, hy_freq, hy_wout, hy_decay, hy_bias, w_ret_o, w_gdn_o, w_hy_o, w_o, norm_ffn, dense_w_in, dense_w_out, moe_router, moe_w_in, moe_w_out, norm_final):
    p = dict(norm_mix=norm_mix, gdn_conv_w=gdn_conv_w, gdn_A_log=gdn_A_log, gdn_dt_bias=gdn_dt_bias,
             gdn_onorm=gdn_onorm, hy_conv_w=hy_conv_w, hy_w1=hy_w1, hy_b1=hy_b1, hy_w2=hy_w2, hy_b2=hy_b2,
             hy_w3=hy_w3, hy_b3=hy_b3, hy_freq=hy_freq, hy_wout=hy_wout, hy_decay=hy_decay, hy_bias=hy_bias,
             norm_ffn=norm_ffn, norm_final=norm_final)
    p['w_in_r'] = jax.vmap(_reorder_w_in)(w_in)
    p['w_ret_o_b'] = w_ret_o.astype(BF16)
    p['w_gdn_o_b'] = w_gdn_o.astype(BF16)
    p['w_hy_o_b'] = w_hy_o.astype(BF16)
    p['w_o_b'] = w_o.astype(BF16)
    p['dense_w_in_b'] = dense_w_in.astype(BF16)
    p['dense_w_out_b'] = dense_w_out.astype(BF16)
    p['moe_router_t'] = jnp.swapaxes(moe_router, 1, 2)
    p['moe_w_in_b'] = moe_w_in.astype(BF16)
    p['moe_w_out_b'] = moe_w_out.astype(BF16)
    return (_trunk(x_prompt, p), _trunk(x_sample, p))
```

```python
import functools
import math

import jax
import jax.numpy as jnp
from jax import lax
from jax.experimental import pallas as pl
from jax.experimental.pallas import tpu as pltpu

F32 = jnp.float32
BF16 = jnp.bfloat16

D_MODEL = 1024
DEPTH = 2
RMS_EPS = 1e-6
RET_HEADS, RET_DK, RET_DV, RET_CHUNK = 4, 128, 128, 128
ROPE_BASE = 10000.0
GDN_HEADS, GDN_DK, GDN_DV, GDN_CHUNK, GDN_CONV = 4, 128, 128, 64, 5
HY_WIDTH, HY_SHORT, HY_EMB, HY_FFN = 512, 3, 33, 64
FF_DENSE = 2816
N_EXPERTS, TOP_K, FF_EXPERT = 8, 2, 3584

COL_GATES = 0
COL_RET = 3072
COL_GDN = 5120
COL_HY = 7168
COL_GAB = 8704
D_PROJ = 8832

VMEM_LIMIT = 56 * 1024 * 1024
HI = lax.Precision.HIGHEST


def _cparams(sem):
    return pltpu.CompilerParams(dimension_semantics=sem, vmem_limit_bytes=VMEM_LIMIT)


def _rms(x, g):
    return x * lax.rsqrt(jnp.mean(x * x, axis=-1, keepdims=True) + RMS_EPS) * g


def _sigmoid(x):
    return 1.0 / (1.0 + jnp.exp(-x))


def _bdot(a, b):
    return jnp.dot(a.astype(BF16), b, preferred_element_type=F32)


def _dot(a, b):
    return jnp.dot(a.astype(BF16), b.astype(BF16), preferred_element_type=F32)


def _dot_nt(a, b):
    return lax.dot_general(a.astype(BF16), b.astype(BF16), (((1,), (1,)), ((), ())), preferred_element_type=F32)


def _proj_body(x_ref, g_ref, w_ref, o_ref, u_ref):
    @pl.when(pl.program_id(1) == 0)
    def _():
        u_ref[...] = _rms(x_ref[...], g_ref[...]).astype(BF16)

    o_ref[...] = jnp.dot(u_ref[...], w_ref[...], preferred_element_type=F32)


def _proj(x, g, w, tm=1024, tn=2944):
    t = x.shape[0]
    return pl.pallas_call(
        _proj_body,
        grid=(t // tm, D_PROJ // tn),
        in_specs=[pl.BlockSpec((tm, D_MODEL), lambda i, j: (i, 0)),
                  pl.BlockSpec((1, D_MODEL), lambda i, j: (0, 0)),
                  pl.BlockSpec((D_MODEL, tn), lambda i, j: (0, j))],
        out_specs=pl.BlockSpec((tm, tn), lambda i, j: (i, j)),
        out_shape=jax.ShapeDtypeStruct((t, D_PROJ), F32),
        scratch_shapes=[pltpu.VMEM((tm, D_MODEL), BF16)],
        compiler_params=_cparams(("parallel", "arbitrary")),
        name="in_proj",
    )(x, g, w)


def _merge_body(x_ref, yr_ref, yg_ref, yh_ref, gr_ref, gg_ref, gh_ref, wr_ref, wg_ref, wh_ref, wo_ref, o_ref):
    merged = (_sigmoid(gr_ref[...]) * _bdot(yr_ref[...], wr_ref[...])
              + _sigmoid(gg_ref[...]) * _bdot(yg_ref[...], wg_ref[...])
              + _sigmoid(gh_ref[...]) * _bdot(yh_ref[...], wh_ref[...]))
    o_ref[...] = x_ref[...] + _bdot(merged, wo_ref[...])


def _merge(x, proj, y_ret, y_gdn, y_hy, w_ret, w_gdn, w_hy, w_o, tm=512):
    t = x.shape[0]
    row = lambda i: (i, 0)
    const = lambda i: (0, 0)
    gate = lambda b: pl.BlockSpec((tm, D_MODEL), lambda i: (i, COL_GATES // D_MODEL + b))
    return pl.pallas_call(
        _merge_body,
        grid=(t // tm,),
        in_specs=[pl.BlockSpec((tm, D_MODEL), row),
                  pl.BlockSpec((tm, 512), row), pl.BlockSpec((tm, 512), row), pl.BlockSpec((tm, 512), row),
                  gate(0), gate(1), gate(2),
                  pl.BlockSpec((512, D_MODEL), const), pl.BlockSpec((512, D_MODEL), const),
                  pl.BlockSpec((512, D_MODEL), const), pl.BlockSpec((D_MODEL, D_MODEL), const)],
        out_specs=pl.BlockSpec((tm, D_MODEL), row),
        out_shape=jax.ShapeDtypeStruct((t, D_MODEL), F32),
        compiler_params=_cparams(("parallel",)),
        name="merge",
    )(x, y_ret, y_gdn, y_hy, proj, proj, proj, w_ret, w_gdn, w_hy, w_o)


def _ffn_body(x_ref, g_ref, wa_ref, wb_ref, wo_ref, o_ref, u_ref, acc_ref):
    j = pl.program_id(1)

    @pl.when(j == 0)
    def _():
        u_ref[...] = _rms(x_ref[...], g_ref[...]).astype(BF16)
        acc_ref[...] = jnp.zeros_like(acc_ref)

    u = u_ref[...]
    a = jnp.dot(u, wa_ref[...], preferred_element_type=F32)
    b = jnp.dot(u, wb_ref[...], preferred_element_type=F32)
    acc_ref[...] += _bdot(a * _sigmoid(a) * b, wo_ref[...])

    @pl.when(j == pl.num_programs(1) - 1)
    def _():
        o_ref[...] = x_ref[...] + acc_ref[...]


def _ffn(x, g, w_in, w_out, tm=512, tf=1408):
    t = x.shape[0]
    nf = FF_DENSE // tf
    return pl.pallas_call(
        _ffn_body,
        grid=(t // tm, nf),
        in_specs=[pl.BlockSpec((tm, D_MODEL), lambda i, j: (i, 0)),
                  pl.BlockSpec((1, D_MODEL), lambda i, j: (0, 0)),
                  pl.BlockSpec((D_MODEL, tf), lambda i, j: (0, j)),
                  pl.BlockSpec((D_MODEL, tf), lambda i, j: (0, j + nf)),
                  pl.BlockSpec((tf, D_MODEL), lambda i, j: (j, 0))],
        out_specs=pl.BlockSpec((tm, D_MODEL), lambda i, j: (i, 0)),
        out_shape=jax.ShapeDtypeStruct((t, D_MODEL), F32),
        scratch_shapes=[pltpu.VMEM((tm, D_MODEL), BF16), pltpu.VMEM((tm, D_MODEL), F32)],
        compiler_params=_cparams(("parallel", "arbitrary")),
        name="dense_ffn",
    )(x, g, w_in, w_in, w_out)


def _route_body(x_ref, g_ref, wr_ref, u_ref, idx_ref, wt_ref):
    u = _rms(x_ref[...], g_ref[...])
    u_ref[...] = u.astype(BF16)
    logits = lax.dot_general(wr_ref[...], u, (((1,), (1,)), ((), ())), precision=HI, preferred_element_type=F32)
    row = lax.broadcasted_iota(jnp.int32, logits.shape, 0)
    m1 = jnp.max(logits, axis=0, keepdims=True)
    i1 = jnp.min(jnp.where(logits == m1, row, N_EXPERTS), axis=0, keepdims=True)
    rest = jnp.where(row == i1, -jnp.inf, logits)
    m2 = jnp.max(rest, axis=0, keepdims=True)
    i2 = jnp.min(jnp.where(rest == m2, row, N_EXPERTS), axis=0, keepdims=True)
    e = jnp.exp(m2 - m1)
    idx_ref[0:1, :] = i1
    idx_ref[1:2, :] = i2
    wt_ref[0:1, :] = 1.0 / (1.0 + e)
    wt_ref[1:2, :] = e / (1.0 + e)


def _route(x, g, wr_t, tm=512):
    t = x.shape[0]
    return pl.pallas_call(
        _route_body,
        grid=(t // tm,),
        in_specs=[pl.BlockSpec((tm, D_MODEL), lambda i: (i, 0)),
                  pl.BlockSpec((1, D_MODEL), lambda i: (0, 0)),
                  pl.BlockSpec((N_EXPERTS, D_MODEL), lambda i: (0, 0))],
        out_specs=[pl.BlockSpec((tm, D_MODEL), lambda i: (i, 0)),
                   pl.BlockSpec((TOP_K, tm), lambda i: (0, i)),
                   pl.BlockSpec((TOP_K, tm), lambda i: (0, i))],
        out_shape=[jax.ShapeDtypeStruct((t, D_MODEL), BF16),
                   jax.ShapeDtypeStruct((TOP_K, t), jnp.int32),
                   jax.ShapeDtypeStruct((TOP_K, t), F32)],
        compiler_params=_cparams(("parallel",)),
        name="moe_router",
    )(x, g, wr_t)


def _experts_body(te_ref, tv_ref, x_ref, sw_ref, wa_ref, wb_ref, wo_ref, o_ref, acc_ref):
    i, j = pl.program_id(0), pl.program_id(1)

    @pl.when(j == 0)
    def _():
        acc_ref[...] = jnp.zeros_like(acc_ref)

    @pl.when(tv_ref[i] != 0)
    def _():
        x = x_ref[...]
        a = jnp.dot(x, wa_ref[...], preferred_element_type=F32)
        b = jnp.dot(x, wb_ref[...], preferred_element_type=F32)
        acc_ref[...] += _bdot(a * _sigmoid(a) * b, wo_ref[...])

    @pl.when(j == pl.num_programs(1) - 1)
    def _():
        o_ref[...] = acc_ref[...] * sw_ref[...]


def _experts(tile_expert, tile_valid, xs, sw, w_in, w_out, tm, tf=512):
    p = xs.shape[0]
    nf = FF_EXPERT // tf
    grid_spec = pltpu.PrefetchScalarGridSpec(
        num_scalar_prefetch=2,
        grid=(p // tm, nf),
        in_specs=[pl.BlockSpec((tm, D_MODEL), lambda i, j, te, tv: (i, 0)),
                  pl.BlockSpec((tm, 1), lambda i, j, te, tv: (i, 0)),
                  pl.BlockSpec((None, D_MODEL, tf), lambda i, j, te, tv: (te[i], 0, j)),
                  pl.BlockSpec((None, D_MODEL, tf), lambda i, j, te, tv: (te[i], 0, j + nf)),
                  pl.BlockSpec((None, tf, D_MODEL), lambda i, j, te, tv: (te[i], j, 0))],
        out_specs=pl.BlockSpec((tm, D_MODEL), lambda i, j, te, tv: (i, 0)),
        scratch_shapes=[pltpu.VMEM((tm, D_MODEL), F32)],
    )
    return pl.pallas_call(
        _experts_body,
        grid_spec=grid_spec,
        out_shape=jax.ShapeDtypeStruct((p, D_MODEL), F32),
        compiler_params=_cparams(("parallel", "arbitrary")),
        name="moe_experts",
    )(tile_expert, tile_valid, xs, sw, w_in, w_in, w_out)


def _moe(x, g, wr_t, w_in, w_out, tm=1024):
    t = x.shape[0]
    u, idx, wts = _route(x, g, wr_t)
    e_flat = idx.reshape(-1)
    onehot = (e_flat[:, None] == jnp.arange(N_EXPERTS, dtype=jnp.int32)[None, :]).astype(jnp.int32)
    csum = jnp.cumsum(onehot, axis=0)
    rank = jnp.sum((csum - onehot) * onehot, axis=1)
    counts = csum[-1]
    padded = ((counts + tm - 1) // tm) * tm
    ends = jnp.cumsum(padded)
    starts = ends - padded
    dest = starts[e_flat] + rank
    p = TOP_K * t + N_EXPERTS * tm
    tok = jnp.tile(jnp.arange(t, dtype=jnp.int32), TOP_K)
    pair = jnp.stack([tok, lax.bitcast_convert_type(wts.reshape(-1), jnp.int32)], axis=1)
    sorted_pair = jnp.zeros((p, 2), jnp.int32).at[dest].set(pair)
    sorted_tok = sorted_pair[:, 0]
    sorted_w = lax.bitcast_convert_type(sorted_pair[:, 1], F32)
    tile_start = jnp.arange(p // tm, dtype=jnp.int32) * tm
    tile_valid = (tile_start < ends[-1]).astype(jnp.int32)
    tile_expert = jnp.minimum(jnp.sum((tile_start[:, None] >= ends[None, :]).astype(jnp.int32), axis=1),
                              N_EXPERTS - 1)
    last_e = jnp.max(jnp.where(counts > 0, jnp.arange(N_EXPERTS, dtype=jnp.int32), 0))
    tile_expert = jnp.where(tile_valid != 0, tile_expert, last_e)
    xs = jnp.take(u, sorted_tok, axis=0)
    ys = _experts(tile_expert, tile_valid, xs, sorted_w[:, None], w_in, w_out, tm)
    dest2 = dest.reshape(TOP_K, t)
    return x + jnp.take(ys, dest2[0], axis=0) + jnp.take(ys, dest2[1], axis=0)


def _norm_body(x_ref, g_ref, o_ref):
    o_ref[...] = _rms(x_ref[...], g_ref[...])


def _final_norm(x, g, tm=1024):
    t = x.shape[0]
    return pl.pallas_call(
        _norm_body,
        grid=(t // tm,),
        in_specs=[pl.BlockSpec((tm, D_MODEL), lambda i: (i, 0)), pl.BlockSpec((1, D_MODEL), lambda i: (0, 0))],
        out_specs=pl.BlockSpec((tm, D_MODEL), lambda i: (i, 0)),
        out_shape=jax.ShapeDtypeStruct((t, D_MODEL), F32),
        compiler_params=_cparams(("parallel",)),
        name="final_norm",
    )(x, g)


RET_GROUP = 4


def _ret_tables(l):
    half = RET_DK // 2
    inv_freq = ROPE_BASE ** (-jnp.arange(half, dtype=F32) / half)
    ang = jnp.arange(l, dtype=F32)[:, None] * inv_freq[None, :]
    cos, sin = jnp.cos(ang), jnp.sin(ang)
    cosf = jnp.concatenate([cos, cos], axis=1)
    sinf = jnp.concatenate([-sin, sin], axis=1)
    c = RET_CHUNK
    log_gamma = jnp.log(1.0 - jnp.exp2(-5.0 - jnp.arange(RET_HEADS, dtype=F32)))[:, None, None]
    idx = jnp.arange(c, dtype=F32)
    dmat = jnp.exp(jnp.abs(idx[:, None] - idx[None, :])[None] * log_gamma)
    col = lambda e: jnp.broadcast_to(jnp.exp(e[None, :, None] * log_gamma), (RET_HEADS, c, 128))
    sc = jnp.stack([col(idx + 1.0), col(c - idx), col(c - 1.0 - idx), col(idx),
                    jnp.broadcast_to(jnp.exp(c * log_gamma), (RET_HEADS, c, 128))], axis=1)
    return cosf, sinf, dmat, sc


def _ret_body(q_ref, k_ref, v_ref, g_ref, cos_ref, sin_ref, dm_ref, sc_ref, o_ref, qs, qd, ks, vs, uf, ub):
    c = RET_CHUNK
    n_chunks = q_ref.shape[0] // c
    q_in, q_out, k_in, k_out = sc_ref[0], sc_ref[1], sc_ref[2], sc_ref[3]
    chunk_decay = sc_ref[4]

    def rot(x, cos, sin):
        return x * cos + pltpu.roll(x, RET_DK // 2, 1) * sin

    grp = RET_GROUP

    def prep(i, carry):
        ks_f, ks_b, vs_ = [], [], []
        for j in range(grp):
            rows = pl.ds(pl.multiple_of((i * grp + j) * c, c), c)
            cos, sin = cos_ref[rows, :], sin_ref[rows, :]
            q = rot(q_ref[rows, :], cos, sin)
            k = rot(k_ref[rows, :], cos, sin) * (RET_DK ** -0.5)
            v = v_ref[rows, :].astype(BF16)
            qs[rows, :] = q.astype(BF16)
            qd[rows, :] = jnp.concatenate([q * q_in, q * q_out], axis=1).astype(BF16)
            ks[rows, :] = k.astype(BF16)
            vs[rows, :] = v
            ks_f.append((k * k_in).T.astype(BF16))
            ks_b.append((k * k_out).T.astype(BF16))
            vs_.append(v)
        for j in range(grp):
            uf[i * grp + j] = jnp.dot(ks_f[j], vs_[j], preferred_element_type=F32)
            ub[i * grp + j] = jnp.dot(ks_b[j], vs_[j], preferred_element_type=F32)
        return carry

    lax.fori_loop(0, n_chunks // grp, prep, 0)

    def fwd_state(n, s):
        u = uf[n]
        uf[n] = s
        return s * chunk_decay + u

    lax.fori_loop(0, n_chunks, fwd_state, jnp.zeros((RET_DK, RET_DV), F32))

    def bwd_state(i, s):
        n = n_chunks - 1 - i
        u = ub[n]
        ub[n] = s
        return s * chunk_decay + u

    lax.fori_loop(0, n_chunks, bwd_state, jnp.zeros((RET_DK, RET_DV), F32))

    def out(i, carry):
        rows = [pl.ds(pl.multiple_of((i * grp + j) * c, c), c) for j in range(grp)]
        ss = [_dot_nt(qs[r, :], ks[r, :]) for r in rows]
        ss = [(s * dm_ref[...]).astype(BF16) for s in ss]
        states = [jnp.concatenate([uf[i * grp + j], ub[i * grp + j]], axis=0).astype(BF16) for j in range(grp)]
        os_ = [jnp.dot(s, vs[r, :], preferred_element_type=F32) for s, r in zip(ss, rows)]
        os_ = [o + jnp.dot(qd[r, :], st, preferred_element_type=F32) for o, r, st in zip(os_, rows, states)]
        for o, r in zip(os_, rows):
            mu = jnp.mean(o, axis=-1, keepdims=True)
            d = o - mu
            o = d * lax.rsqrt(jnp.mean(d * d, axis=-1, keepdims=True) + RMS_EPS)
            g = g_ref[r, :]
            o_ref[r, :] = g * _sigmoid(g) * o
        return carry

    lax.fori_loop(0, n_chunks // grp, out, 0)


def retention(proj, b, l, col0):
    cosf, sinf, dmat, sc = _ret_tables(l)
    cb = col0 // 128
    nh = RET_HEADS
    c = RET_CHUNK
    seq = lambda off: pl.BlockSpec((l, 128), lambda i, h: (i, cb + off * nh + h))
    const2 = pl.BlockSpec((l, 128), lambda i, h: (0, 0))
    return pl.pallas_call(
        _ret_body,
        grid=(b, nh),
        in_specs=[seq(0), seq(1), seq(2), seq(3), const2, const2,
                  pl.BlockSpec((None, c, c), lambda i, h: (h, 0, 0)),
                  pl.BlockSpec((None, 5, c, 128), lambda i, h: (h, 0, 0, 0))],
        out_specs=pl.BlockSpec((l, 128), lambda i, h: (i, h)),
        out_shape=jax.ShapeDtypeStruct((b * l, nh * RET_DV), F32),
        scratch_shapes=[pltpu.VMEM((l, 128), BF16), pltpu.VMEM((l, 256), BF16), pltpu.VMEM((l, 128), BF16),
                        pltpu.VMEM((l, 128), BF16), pltpu.VMEM((l // c, 128, 128), F32),
                        pltpu.VMEM((l // c, 128, 128), F32)],
        compiler_params=_cparams(("parallel", "parallel")),
        name="retention",
    )(proj, proj, proj, proj, cosf, sinf, dmat, sc)


GDN_ROWS = 512
HALO = 8


def _cumsum_rows(x, reverse):
    n = x.shape[0]
    row = lax.broadcasted_iota(jnp.int32, x.shape, 0)
    s = 1
    while s < n:
        if reverse:
            x = x + jnp.where(row < n - s, pltpu.roll(x, n - s, 0), 0.0)
        else:
            x = x + jnp.where(row >= s, pltpu.roll(x, s, 0), 0.0)
        s *= 2
    return x


def _gdn_local(insts, neg_a, dtb):
    c = GDN_CHUNK
    row = lax.broadcasted_iota(jnp.int32, (c, c), 0)
    col = lax.broadcasted_iota(jnp.int32, (c, c), 1)
    pre = []
    for q, k, v, graw, braw, d in insts:
        reverse = d == 1
        x = graw + dtb[d]
        softplus = jnp.maximum(x, 0.0) + jnp.log(1.0 + jnp.exp(-jnp.abs(x)))
        log_g = neg_a[d] * softplus
        beta = _sigmoid(braw)
        cum = _cumsum_rows(jnp.broadcast_to(log_g, (c, 128)), reverse)
        total = cum[0:1, :] if reverse else cum[c - 1:c, :]
        cum_sq = cum[:, :c]
        cum_row = jnp.sum(jnp.where(row == col, cum_sq, 0.0), axis=0, keepdims=True)
        incl = (row <= col) if reverse else (row >= col)
        strict = (row < col) if reverse else (row > col)
        decay = jnp.where(incl, jnp.exp(jnp.where(incl, cum_sq - cum_row, 0.0)), 0.0)
        pre.append((cum, total, decay, strict, k * beta, v * beta))
    qks = [_dot_nt(jnp.concatenate([q, p[4]], axis=0), k) for (q, k, *_), p in zip(insts, pre)]
    bps = [-jnp.where(p[3], qk[c:] * p[2], 0.0) for qk, p in zip(qks, pre)]
    rs = list(bps)
    for _ in range(5):
        bps = [_dot(bp, bp) for bp in bps]
        rs = [r + bp + _dot(r, bp) for r, bp in zip(rs, bps)]
    rhss = [jnp.concatenate([p[5], p[4] * jnp.exp(p[0])], axis=1) for p in pre]
    sols = [rhs + _dot(r, rhs) for r, rhs in zip(rs, rhss)]
    kdts, attns = [], []
    for (q, k, *_), p, qk in zip(insts, pre, qks):
        cum, total, decay = p[0], p[1], p[2]
        k_dec = k * jnp.exp(total - cum)
        kdts.append(jnp.concatenate([k_dec, jnp.zeros((c, 128), F32)], axis=0).T[:, :c])
        attns.append(qk[:c] * decay)
    kus = [_dot(kd_t, sol) for kd_t, sol in zip(kdts, sols)]
    aos = [_dot(attn, sol) for attn, sol in zip(attns, sols)]
    outs = []
    for (q, *_), p, ku, ao in zip(insts, pre, kus, aos):
        q_eff = q * jnp.exp(p[0]) - ao[:, GDN_DV:]
        outs.append((jnp.concatenate([ku[:, GDN_DV:], q_eff], axis=0), ku[:, :GDN_DV], ao[:, :GDN_DV], jnp.exp(p[1])))
    return outs


def _gdn_body(alog_ref, dtb_ref, q_ref, k_ref, v_ref, gate_ref, gab_ref, cwq_ref, cwk_ref, cwv_ref, onorm_ref,
              o_ref, pad, km_s, u_s, o0_s, gl_s, sf, sb):
    c = GDN_CHUNK
    l = q_ref.shape[0]
    h = pl.program_id(1)
    n_chunks = l // c
    per_block = GDN_ROWS // c

    zero_halo = jnp.zeros((HALO, 128), F32)
    for a, src in enumerate((q_ref, k_ref, v_ref)):
        pad[a, 0:HALO, :] = zero_halo
        pad[a, l + HALO:l + 2 * HALO, :] = zero_halo
        pad[a, HALO:l + HALO, :] = src[...]

    ones = jnp.ones((c, 1), F32)
    neg_a = (-jnp.exp(ones * alog_ref[0, h]), -jnp.exp(ones * alog_ref[1, h]))
    dtb = (dtb_ref[0, h], dtb_ref[1, h])
    win_rows = GDN_ROWS + 2 * HALO

    def local_block(i, carry):
        s0 = pl.multiple_of(i * GDN_ROWS, GDN_ROWS)
        ys = []
        for a, cw_ref in enumerate((cwq_ref, cwk_ref, cwv_ref)):
            win = pad[a, pl.ds(s0, win_rows), :]
            acc = cw_ref[2:3, :] * win
            for j in (0, 1, 3, 4):
                acc = acc + cw_ref[j:j + 1, :] * pltpu.roll(win, (GDN_CONV // 2 - j) % win_rows, 0)
            y = acc[HALO:HALO + GDN_ROWS]
            y = y * _sigmoid(y)
            if a < 2:
                y = y * lax.rsqrt(jnp.sum(y * y, axis=-1, keepdims=True) + 1e-6)
            if a == 0:
                y = y * (GDN_DK ** -0.5)
            ys.append(y)
        ab = gab_ref[pl.ds(s0, GDN_ROWS), :]
        insts = []
        for j in range(per_block):
            sl = slice(j * c, (j + 1) * c)
            for d in range(2):
                insts.append((ys[0][sl], ys[1][sl], ys[2][sl], ab[sl, d:d + 1], ab[sl, 2 + d:3 + d], d))
        outs = _gdn_local(insts, neg_a, dtb)
        for idx, (km, u, o0, g_last) in enumerate(outs):
            m = (idx % 2) * n_chunks + i * per_block + idx // 2
            km_s[m] = km.astype(BF16)
            u_s[m] = u.astype(BF16)
            o0_s[m] = o0
            gl_s[m] = jnp.broadcast_to(g_last, (8, 128))
        return carry

    lax.fori_loop(0, l // GDN_ROWS, local_block, 0)

    sf[...] = jnp.zeros_like(sf)
    sb[...] = jnp.zeros_like(sb)

    def scan(i, carry):
        for d, s_ref in ((0, sf), (1, sb)):
            n = i if d == 0 else n_chunks - 1 - i
            m = d * n_chunks + n
            rows = pl.ds(pl.multiple_of(n * c, c) + HALO, c)
            s = s_ref[...]
            r = jnp.dot(km_s[m], s.astype(BF16), preferred_element_type=F32)
            pad[d, rows, :] = o0_s[m] + r[GDN_DK:]
            s_ref[...] = s * gl_s[m][0:1, :] - r[:GDN_DK] + u_s[m].astype(F32)
        return carry

    lax.fori_loop(0, n_chunks, scan, 0)

    def fin(i, carry):
        r0 = pl.multiple_of(i * GDN_ROWS, GDN_ROWS)
        o = pad[0, pl.ds(r0 + HALO, GDN_ROWS), :] + pad[1, pl.ds(r0 + HALO, GDN_ROWS), :]
        o = o * lax.rsqrt(jnp.mean(o * o, axis=-1, keepdims=True) + RMS_EPS) * onorm_ref[...]
        g = gate_ref[pl.ds(r0, GDN_ROWS), :]
        o_ref[pl.ds(r0, GDN_ROWS), :] = o * (g * _sigmoid(g))
        return carry

    lax.fori_loop(0, l // GDN_ROWS, fin, 0)


def gdn(proj, gabh, conv_w, a_log, dt_bias, onorm, b, l, col0):
    cb = col0 // 128
    nh = GDN_HEADS
    nc = l // GDN_CHUNK
    seq = lambda off: pl.BlockSpec((l, 128), lambda i, h, *_: (i, cb + off * nh + h))
    cw = lambda off: pl.BlockSpec((GDN_CONV, 128), lambda i, h, *_: (0, off * nh + h))
    grid_spec = pltpu.PrefetchScalarGridSpec(
        num_scalar_prefetch=2,
        grid=(b, nh),
        in_specs=[seq(0), seq(1), seq(2), seq(3),
                  pl.BlockSpec((None, l, 4), lambda i, h, *_: (h, i, 0)),
                  cw(0), cw(1), cw(2),
                  pl.BlockSpec((1, 128), lambda i, h, *_: (0, 0))],
        out_specs=pl.BlockSpec((l, 128), lambda i, h, *_: (i, h)),
        scratch_shapes=[pltpu.VMEM((3, l + 2 * HALO, 128), F32),
                        pltpu.VMEM((2 * nc, GDN_DK + GDN_CHUNK, GDN_DK), BF16),
                        pltpu.VMEM((2 * nc, GDN_DK, GDN_DV), BF16),
                        pltpu.VMEM((2 * nc, GDN_CHUNK, GDN_DV), F32),
                        pltpu.VMEM((2 * nc, 8, 128), F32),
                        pltpu.VMEM((GDN_DK, GDN_DV), F32), pltpu.VMEM((GDN_DK, GDN_DV), F32)],
    )
    return pl.pallas_call(
        _gdn_body,
        grid_spec=grid_spec,
        out_shape=jax.ShapeDtypeStruct((b * l, nh * GDN_DV), F32),
        compiler_params=_cparams(("parallel", "parallel")),
        name="gated_deltanet",
    )(a_log, dt_bias, proj, proj, proj, proj, gabh, conv_w, conv_w, conv_w, onorm[None, :])


HY_N1 = 64
HY_ROWS = 256


def _hyconv_body(x0_ref, x1_ref, v_ref, c0_ref, c1_ref, cv_ref, z_ref, x0c_ref, pad):
    l = x0_ref.shape[0]
    zero_halo = jnp.zeros((HALO, 128), F32)
    for a, src in enumerate((x0_ref, x1_ref, v_ref)):
        pad[a, 0:HALO, :] = zero_halo
        pad[a, l + HALO:l + 2 * HALO, :] = zero_halo
        pad[a, HALO:l + HALO, :] = src[...]
    win_rows = HY_ROWS + 2 * HALO

    def block(i, carry):
        s0 = pl.multiple_of(i * HY_ROWS, HY_ROWS)
        outs = []
        for a, cw_ref in enumerate((c0_ref, c1_ref, cv_ref)):
            win = pad[a, pl.ds(s0, win_rows), :]
            acc = cw_ref[1:2, :] * win
            acc = acc + cw_ref[0:1, :] * pltpu.roll(win, 1, 0)
            acc = acc + cw_ref[2:3, :] * pltpu.roll(win, win_rows - 1, 0)
            outs.append(acc[HALO:HALO + HY_ROWS])
        rows = pl.ds(s0, HY_ROWS)
        x0c_ref[rows, :] = outs[0]
        z_ref[rows, :] = outs[2] * outs[1]
        return carry

    lax.fori_loop(0, l // HY_ROWS, block, 0)


def _hy_conv(proj, conv_w, b, l, col0):
    cb = col0 // 128
    nc = HY_WIDTH // 128
    seq = lambda off: pl.BlockSpec((l, 128), lambda i, c: (i, cb + off * nc + c))
    cw = lambda off: pl.BlockSpec((HY_SHORT, 128), lambda i, c: (0, off * nc + c))
    out = pl.BlockSpec((l, 128), lambda i, c: (i, c))
    return pl.pallas_call(
        _hyconv_body,
        grid=(b, nc),
        in_specs=[seq(0), seq(1), seq(2), cw(0), cw(1), cw(2)],
        out_specs=[out, out],
        out_shape=[jax.ShapeDtypeStruct((b * l, HY_WIDTH), F32), jax.ShapeDtypeStruct((b * l, HY_WIDTH), F32)],
        scratch_shapes=[pltpu.VMEM((3, l + 2 * HALO, 128), F32)],
        compiler_params=_cparams(("parallel", "parallel")),
        name="hyena_short_conv",
    )(proj, proj, proj, conv_w, conv_w, conv_w)


def _hyfilt_body(feat_ref, w1_ref, b1_ref, w2_ref, b2_ref, w3_ref, b3_ref, fr_ref, wo_ref, dec_ref, o_ref, *, l):
    tm = feat_ref.shape[0]
    hdot = lambda a, b_: jnp.dot(a, b_, precision=HI, preferred_element_type=F32)
    feat = feat_ref[...]
    fr = fr_ref[...]
    h = jnp.sin(fr * (hdot(feat, w1_ref[...]) + b1_ref[...]))
    h = jnp.sin(fr * (hdot(h, w2_ref[...]) + b2_ref[...]))
    h = jnp.sin(fr * (hdot(h, w3_ref[...]) + b3_ref[...]))
    h = hdot(h, wo_ref[...]) * jnp.exp(-feat[:, 0:1] * jnp.abs(dec_ref[...]))
    m = pl.program_id(0) * tm + lax.broadcasted_iota(jnp.int32, (tm, 1), 0)
    o_ref[...] = jnp.where(m == l, 0.0, h)


def _hy_filter(l, w1, b1, w2, b2, w3, b3, freq, wout, decay, tm=512):
    n = 2 * l
    m = jnp.arange(n, dtype=jnp.int32)
    t = jnp.where(m < l, m, n - m).astype(F32)[:, None]
    t01 = t / (l - 1)
    bands = (HY_EMB - 1) // 2
    f = jnp.linspace(1e-4, bands - 1, bands, dtype=F32)[None, :]
    w = 2.0 * math.pi * t / l
    feat = jnp.concatenate([t01, jnp.cos(f * w), -jnp.sin(f * w)], axis=-1)
    const = lambda i: (0, 0)
    half = lambda i: (0, (i * tm) // l)
    vec = lambda x: x[None, :]
    return pl.pallas_call(
        functools.partial(_hyfilt_body, l=l),
        grid=(n // tm,),
        in_specs=[pl.BlockSpec((tm, HY_EMB), lambda i: (i, 0)),
                  pl.BlockSpec((HY_EMB, HY_FFN), const), pl.BlockSpec((1, HY_FFN), const),
                  pl.BlockSpec((HY_FFN, HY_FFN), const), pl.BlockSpec((1, HY_FFN), const),
                  pl.BlockSpec((HY_FFN, HY_FFN), const), pl.BlockSpec((1, HY_FFN), const),
                  pl.BlockSpec((1, HY_FFN), const),
                  pl.BlockSpec((HY_FFN, HY_WIDTH), half),
                  pl.BlockSpec((None, 1, HY_WIDTH), lambda i: ((i * tm) // l, 0, 0))],
        out_specs=pl.BlockSpec((tm, HY_WIDTH), lambda i: (i, 0)),
        out_shape=jax.ShapeDtypeStruct((n, HY_WIDTH), F32),
        compiler_params=_cparams(("parallel",)),
        name="hyena_filter",
    )(feat, w1, vec(b1), w2, vec(b2), w3, vec(b3), vec(freq), wout, decay[:, None, :])


def _dft_tables(l):
    n = 2 * l
    n1, n2 = HY_N1, n // HY_N1
    ang = lambda num, den: (2.0 * math.pi / den) * (num % den).astype(F32)
    i1 = jnp.arange(n1, dtype=jnp.int32)
    a1 = ang(i1[:, None] * i1[None, :], n1)
    fwd1 = jnp.concatenate([jnp.cos(a1), -jnp.sin(a1)], axis=0)
    a1h = a1[:, :n1 // 2].T
    inv1 = jnp.concatenate([jnp.cos(a1h), -jnp.sin(a1h)], axis=1) / n
    i2 = jnp.arange(n2, dtype=jnp.int32)
    freq = i1[:, None, None] + n1 * i2[None, :, None]
    a2 = ang(freq * i2[None, None, :], n)
    gr, gi = jnp.cos(a2), -jnp.sin(a2)
    fwd2 = jnp.concatenate([jnp.concatenate([gr, -gi], axis=2), jnp.concatenate([gi, gr], axis=2)], axis=1)
    inv2 = jnp.swapaxes(fwd2, 1, 2)
    return fwd1, inv1, fwd2, inv2


HY_TR = 8


def _dft1_body(x_ref, f_ref, o_ref, *, exact):
    k1, tr, c = x_ref.shape
    x = x_ref[...].reshape(k1 * tr, c)
    if exact:
        o = jnp.dot(f_ref[...], x, precision=HI, preferred_element_type=F32)
    else:
        o = jnp.dot(f_ref[...], x.astype(BF16), preferred_element_type=F32)
    o_ref[...] = o.astype(o_ref.dtype).reshape(o_ref.shape)


def _dft1(x4, mat, out_dtype, exact):
    bt, k1, n2, c = x4.shape
    n1 = mat.shape[0] // 2
    kron = jnp.kron(mat, jnp.eye(HY_TR, dtype=mat.dtype))
    return pl.pallas_call(
        functools.partial(_dft1_body, exact=exact),
        grid=(bt, n2 // HY_TR),
        in_specs=[pl.BlockSpec((None, k1, HY_TR, c), lambda i, j: (i, 0, j, 0)),
                  pl.BlockSpec((2 * n1 * HY_TR, k1 * HY_TR), lambda i, j: (0, 0))],
        out_specs=pl.BlockSpec((None, 2, n1, HY_TR, c), lambda i, j: (i, 0, 0, j, 0)),
        out_shape=jax.ShapeDtypeStruct((bt, 2, n1, n2, c), out_dtype),
        compiler_params=_cparams(("parallel", "parallel")),
        name="hyena_dft_stage1",
    )(x4, kron)


def _spec2_body(a_ref, g_ref, o_ref):
    n2 = a_ref.shape[1]
    a = a_ref[...].reshape(2 * n2, a_ref.shape[2])
    o = jnp.dot(g_ref[...], a, precision=HI, preferred_element_type=F32)
    o_ref[...] = o.reshape(o_ref.shape)


def _filter_spectrum(circ, fwd1, fwd2, l):
    n1, n2 = HY_N1, 2 * l // HY_N1
    a5 = _dft1(circ.reshape(1, n1, n2, HY_WIDTH), fwd1, F32, True)[0]
    return pl.pallas_call(
        _spec2_body,
        grid=(n1,),
        in_specs=[pl.BlockSpec((2, None, n2, HY_WIDTH), lambda i: (0, i, 0, 0)),
                  pl.BlockSpec((None, 2 * n2, 2 * n2), lambda i: (i, 0, 0))],
        out_specs=pl.BlockSpec((None, 2, n2, HY_WIDTH), lambda i: (i, 0, 0, 0)),
        out_shape=jax.ShapeDtypeStruct((n1, 2, n2, HY_WIDTH), F32),
        compiler_params=_cparams(("parallel",)),
        name="hyena_filter_spectrum",
    )(a5, fwd2)


def _hymid_body(a_ref, g_ref, gi_ref, h_ref, o_ref):
    bb, _, n2, c = a_ref.shape
    a = jnp.concatenate([a_ref[i].reshape(2 * n2, c) for i in range(bb)], axis=1)
    x = jnp.dot(g_ref[...], a, preferred_element_type=F32)
    hr = jnp.concatenate([h_ref[0]] * bb, axis=1)
    hi = jnp.concatenate([h_ref[1]] * bb, axis=1)
    xr, xi = x[:n2], x[n2:]
    y = jnp.concatenate([xr * hr - xi * hi, xr * hi + xi * hr], axis=0).astype(BF16)
    o = jnp.dot(gi_ref[...], y, preferred_element_type=F32).astype(o_ref.dtype)
    for i in range(bb):
        o_ref[i] = o[:, i * c:(i + 1) * c].reshape(2, n2, c)


def _hy_mid(a5, fwd2, inv2, spec, bb=4):
    b, _, n1, n2, c = a5.shape
    blk = pl.BlockSpec((bb, 2, None, n2, c), lambda f, i: (i, 0, f, 0, 0))
    mat = pl.BlockSpec((None, 2 * n2, 2 * n2), lambda f, i: (f, 0, 0))
    return pl.pallas_call(
        _hymid_body,
        grid=(n1, b // bb),
        in_specs=[blk, mat, mat, pl.BlockSpec((None, 2, n2, c), lambda f, i: (f, 0, 0, 0))],
        out_specs=blk,
        out_shape=jax.ShapeDtypeStruct(a5.shape, BF16),
        compiler_params=_cparams(("parallel", "parallel")),
        name="hyena_spectral_product",
    )(a5, fwd2, inv2, spec)


def _hyfin_body(b_ref, m_ref, z_ref, x0_ref, bias_ref, o_ref):
    k1, tr, c = z_ref.shape
    y = jnp.dot(m_ref[...], b_ref[...].reshape(m_ref.shape[1], c), preferred_element_type=F32).reshape(k1, tr, c)
    o_ref[...] = (y + z_ref[...] * bias_ref[...]) * x0_ref[...]


def _hy_final(bq5, inv1, z4, x04, bias):
    b, _, n1, n2, c = bq5.shape
    k1 = z4.shape[1]
    kron = jnp.kron(inv1, jnp.eye(HY_TR, dtype=inv1.dtype))
    row = pl.BlockSpec((None, k1, HY_TR, c), lambda i, j: (i, 0, j, 0))
    return pl.pallas_call(
        _hyfin_body,
        grid=(b, n2 // HY_TR),
        in_specs=[pl.BlockSpec((None, 2, n1, HY_TR, c), lambda i, j: (i, 0, 0, j, 0)),
                  pl.BlockSpec((k1 * HY_TR, 2 * n1 * HY_TR), lambda i, j: (0, 0)), row, row,
                  pl.BlockSpec((1, 1, c), lambda i, j: (0, 0, 0))],
        out_specs=row,
        out_shape=jax.ShapeDtypeStruct(z4.shape, F32),
        compiler_params=_cparams(("parallel", "parallel")),
        name="hyena_dft_final",
    )(bq5, kron, z4, x04, bias[None, None, :])


def hyena(proj, conv_w, w1, b1, w2, b2, w3, b3, freq, wout, decay, bias, b, l, col0):
    n1, n2, c = HY_N1, 2 * l // HY_N1, HY_WIDTH
    fwd1, inv1, fwd2, inv2 = _dft_tables(l)
    circ = _hy_filter(l, w1, b1, w2, b2, w3, b3, freq, wout, decay)
    spec = _filter_spectrum(circ, fwd1, fwd2, l)
    z, x0c = _hy_conv(proj, conv_w, b, l, col0)
    z4 = z.reshape(b, n1 // 2, n2, c)
    a5 = _dft1(z4, fwd1[:, :n1 // 2].astype(BF16), BF16, False)
    bq5 = _hy_mid(a5, fwd2.astype(BF16), inv2.astype(BF16), spec)
    out = _hy_final(bq5, inv1.astype(BF16), z4, x0c.reshape(b, n1 // 2, n2, c), bias)
    return out.reshape(b * l, c)


def _reorder_w_in(w):
    rest, gab, hy, gates = w[:, :4096], w[:, 4096:4112], w[:, 4112:5648], w[:, 5648:]
    pad = jnp.zeros((D_MODEL, D_PROJ - COL_GAB - 16), w.dtype)
    return jnp.concatenate([gates, rest, hy, gab, pad], axis=1).astype(BF16)


def _trunk(x3, p):
    b, l, _ = x3.shape
    t = b * l
    x = x3.reshape(t, D_MODEL)
    for i in range(DEPTH):
        proj = _proj(x, p['norm_mix'][i][None, :], p['w_in_r'][i])
        y_ret = retention(proj, b, l, COL_RET)
        gabh = jnp.transpose(proj[:, COL_GAB:COL_GAB + 4 * GDN_HEADS].reshape(t, 4, GDN_HEADS), (2, 0, 1))
        y_gdn = gdn(proj, gabh, p['gdn_conv_w'][i], p['gdn_A_log'][i], p['gdn_dt_bias'][i], p['gdn_onorm'][i],
                    b, l, COL_GDN)
        y_hy = hyena(proj, p['hy_conv_w'][i], p['hy_w1'][i], p['hy_b1'][i], p['hy_w2'][i], p['hy_b2'][i],
                     p['hy_w3'][i], p['hy_b3'][i], p['hy_freq'][i], p['hy_wout'][i], p['hy_decay'][i],
                     p['hy_bias'][i], b, l, COL_HY)
        x = _merge(x, proj, y_ret, y_gdn, y_hy, p['w_ret_o_b'][i], p['w_gdn_o_b'][i], p['w_hy_o_b'][i], p['w_o_b'][i])
        g = p['norm_ffn'][i][None, :]
        if i % 2 == 0:
            x = _ffn(x, g, p['dense_w_in_b'][i // 2], p['dense_w_out_b'][i // 2])
        else:
            x = _moe(x, g, p['moe_router_t'][i // 2], p['moe_w_in_b'][i // 2], p['moe_w_out_b'][i // 2])
    return _final_norm(x, p['norm_final'][None, :]).reshape(b, l, D_MODEL)


def kernel(x_prompt, x_sample, norm_mix, w_in, gdn_conv_w, gdn_A_log, gdn_dt_bias, gdn_onorm, hy_conv_w, hy_w1, hy_b1, hy_w2, hy_b2, hy_w3, hy_b3, hy_freq, hy_wout, hy_decay, hy_bias, w_ret_o, w_gdn_o, w_hy_o, w_o, norm_ffn, dense_w_in, dense_w_out, moe_router, moe_w_in, moe_w_out, norm_final):
    p = dict(norm_mix=norm_mix, gdn_conv_w=gdn_conv_w, gdn_A_log=gdn_A_log, gdn_dt_bias=gdn_dt_bias,
             gdn_onorm=gdn_onorm, hy_conv_w=hy_conv_w, hy_w1=hy_w1, hy_b1=hy_b1, hy_w2=hy_w2, hy_b2=hy_b2,
             hy_w3=hy_w3, hy_b3=hy_b3, hy_freq=hy_freq, hy_wout=hy_wout, hy_decay=hy_decay, hy_bias=hy_bias,
             norm_ffn=norm_ffn, norm_final=norm_final)
    p['w_in_r'] = jax.vmap(_reorder_w_in)(w_in)
    p['w_ret_o_b'] = w_ret_o.astype(BF16)
    p['w_gdn_o_b'] = w_gdn_o.astype(BF16)
    p['w_hy_o_b'] = w_hy_o.astype(BF16)
    p['w_o_b'] = w_o.astype(BF16)
    p['dense_w_in_b'] = dense_w_in.astype(BF16)
    p['dense_w_out_b'] = dense_w_out.astype(BF16)
    p['moe_router_t'] = jnp.swapaxes(moe_router, 1, 2)
    p['moe_w_in_b'] = moe_w_in.astype(BF16)
    p['moe_w_out_b'] = moe_w_out.astype(BF16)
    return (_trunk(x_prompt, p), _trunk(x_sample, p))
```

```python
import functools
import math

import jax
import jax.numpy as jnp
from jax import lax
from jax.experimental import pallas as pl
from jax.experimental.pallas import tpu as pltpu

F32 = jnp.float32
BF16 = jnp.bfloat16

D_MODEL = 1024
DEPTH = 2
RMS_EPS = 1e-6
RET_HEADS, RET_DK, RET_DV, RET_CHUNK = 4, 128, 128, 128
ROPE_BASE = 10000.0
GDN_HEADS, GDN_DK, GDN_DV, GDN_CHUNK, GDN_CONV = 4, 128, 128, 64, 5
HY_WIDTH, HY_SHORT, HY_EMB, HY_FFN = 512, 3, 33, 64
FF_DENSE = 2816
N_EXPERTS, TOP_K, FF_EXPERT = 8, 2, 3584

COL_GATES = 0
COL_RET = 3072
COL_GDN = 5120
COL_HY = 7168
COL_GAB = 8704
D_PROJ = 8832

VMEM_LIMIT = 56 * 1024 * 1024
HI = lax.Precision.HIGHEST


def _cparams(sem):
    return pltpu.CompilerParams(dimension_semantics=sem, vmem_limit_bytes=VMEM_LIMIT)


def _rms(x, g):
    return x * lax.rsqrt(jnp.mean(x * x, axis=-1, keepdims=True) + RMS_EPS) * g


def _sigmoid(x):
    return 1.0 / (1.0 + jnp.exp(-x))


def _bdot(a, b):
    return jnp.dot(a.astype(BF16), b, preferred_element_type=F32)


def _dot(a, b):
    return jnp.dot(a.astype(BF16), b.astype(BF16), preferred_element_type=F32)


def _dot_nt(a, b):
    return lax.dot_general(a.astype(BF16), b.astype(BF16), (((1,), (1,)), ((), ())), preferred_element_type=F32)


def _proj_body(x_ref, g_ref, w_ref, o_ref, u_ref):
    @pl.when(pl.program_id(1) == 0)
    def _():
        u_ref[...] = _rms(x_ref[...], g_ref[...]).astype(BF16)

    o_ref[...] = jnp.dot(u_ref[...], w_ref[...], preferred_element_type=F32)


def _proj(x, g, w, tm=1024, tn=2944):
    t = x.shape[0]
    return pl.pallas_call(
        _proj_body,
        grid=(t // tm, D_PROJ // tn),
        in_specs=[pl.BlockSpec((tm, D_MODEL), lambda i, j: (i, 0)),
                  pl.BlockSpec((1, D_MODEL), lambda i, j: (0, 0)),
                  pl.BlockSpec((D_MODEL, tn), lambda i, j: (0, j))],
        out_specs=pl.BlockSpec((tm, tn), lambda i, j: (i, j)),
        out_shape=jax.ShapeDtypeStruct((t, D_PROJ), F32),
        scratch_shapes=[pltpu.VMEM((tm, D_MODEL), BF16)],
        compiler_params=_cparams(("parallel", "arbitrary")),
        name="in_proj",
    )(x, g, w)


def _merge_body(x_ref, yr_ref, yg_ref, yh_ref, gr_ref, gg_ref, gh_ref, wr_ref, wg_ref, wh_ref, wo_ref, o_ref):
    merged = (_sigmoid(gr_ref[...]) * _bdot(yr_ref[...], wr_ref[...])
              + _sigmoid(gg_ref[...]) * _bdot(yg_ref[...], wg_ref[...])
              + _sigmoid(gh_ref[...]) * _bdot(yh_ref[...], wh_ref[...]))
    o_ref[...] = x_ref[...] + _bdot(merged, wo_ref[...])


def _merge(x, proj, y_ret, y_gdn, y_hy, w_ret, w_gdn, w_hy, w_o, tm=512):
    t = x.shape[0]
    row = lambda i: (i, 0)
    const = lambda i: (0, 0)
    gate = lambda b: pl.BlockSpec((tm, D_MODEL), lambda i: (i, COL_GATES // D_MODEL + b))
    return pl.pallas_call(
        _merge_body,
        grid=(t // tm,),
        in_specs=[pl.BlockSpec((tm, D_MODEL), row),
                  pl.BlockSpec((tm, 512), row), pl.BlockSpec((tm, 512), row), pl.BlockSpec((tm, 512), row),
                  gate(0), gate(1), gate(2),
                  pl.BlockSpec((512, D_MODEL), const), pl.BlockSpec((512, D_MODEL), const),
                  pl.BlockSpec((512, D_MODEL), const), pl.BlockSpec((D_MODEL, D_MODEL), const)],
        out_specs=pl.BlockSpec((tm, D_MODEL), row),
        out_shape=jax.ShapeDtypeStruct((t, D_MODEL), F32),
        compiler_params=_cparams(("parallel",)),
        name="merge",
    )(x, y_ret, y_gdn, y_hy, proj, proj, proj, w_ret, w_gdn, w_hy, w_o)


def _ffn_body(x_ref, g_ref, wa_ref, wb_ref, wo_ref, o_ref, u_ref, acc_ref):
    j = pl.program_id(1)

    @pl.when(j == 0)
    def _():
        u_ref[...] = _rms(x_ref[...], g_ref[...]).astype(BF16)
        acc_ref[...] = jnp.zeros_like(acc_ref)

    u = u_ref[...]
    a = jnp.dot(u, wa_ref[...], preferred_element_type=F32)
    b = jnp.dot(u, wb_ref[...], preferred_element_type=F32)
    acc_ref[...] += _bdot(a * _sigmoid(a) * b, wo_ref[...])

    @pl.when(j == pl.num_programs(1) - 1)
    def _():
        o_ref[...] = x_ref[...] + acc_ref[...]


def _ffn(x, g, w_in, w_out, tm=512, tf=1408):
    t = x.shape[0]
    nf = FF_DENSE // tf
    return pl.pallas_call(
        _ffn_body,
        grid=(t // tm, nf),
        in_specs=[pl.BlockSpec((tm, D_MODEL), lambda i, j: (i, 0)),
                  pl.BlockSpec((1, D_MODEL), lambda i, j: (0, 0)),
                  pl.BlockSpec((D_MODEL, tf), lambda i, j: (0, j)),
                  pl.BlockSpec((D_MODEL, tf), lambda i, j: (0, j + nf)),
                  pl.BlockSpec((tf, D_MODEL), lambda i, j: (j, 0))],
        out_specs=pl.BlockSpec((tm, D_MODEL), lambda i, j: (i, 0)),
        out_shape=jax.ShapeDtypeStruct((t, D_MODEL), F32),
        scratch_shapes=[pltpu.VMEM((tm, D_MODEL), BF16), pltpu.VMEM((tm, D_MODEL), F32)],
        compiler_params=_cparams(("parallel", "arbitrary")),
        name="dense_ffn",
    )(x, g, w_in, w_in, w_out)


def _route_body(x_ref, g_ref, wr_ref, u_ref, idx_ref, wt_ref):
    u = _rms(x_ref[...], g_ref[...])
    u_ref[...] = u
    logits = lax.dot_general(wr_ref[...], u, (((1,), (1,)), ((), ())), precision=HI, preferred_element_type=F32)
    row = lax.broadcasted_iota(jnp.int32, logits.shape, 0)
    m1 = jnp.max(logits, axis=0, keepdims=True)
    i1 = jnp.min(jnp.where(logits == m1, row, N_EXPERTS), axis=0, keepdims=True)
    rest = jnp.where(row == i1, -jnp.inf, logits)
    m2 = jnp.max(rest, axis=0, keepdims=True)
    i2 = jnp.min(jnp.where(rest == m2, row, N_EXPERTS), axis=0, keepdims=True)
    e = jnp.exp(m2 - m1)
    idx_ref[0:1, :] = i1
    idx_ref[1:2, :] = i2
    wt_ref[0:1, :] = 1.0 / (1.0 + e)
    wt_ref[1:2, :] = e / (1.0 + e)


def _route(x, g, wr_t, tm=512):
    t = x.shape[0]
    return pl.pallas_call(
        _route_body,
        grid=(t // tm,),
        in_specs=[pl.BlockSpec((tm, D_MODEL), lambda i: (i, 0)),
                  pl.BlockSpec((1, D_MODEL), lambda i: (0, 0)),
                  pl.BlockSpec((N_EXPERTS, D_MODEL), lambda i: (0, 0))],
        out_specs=[pl.BlockSpec((tm, D_MODEL), lambda i: (i, 0)),
                   pl.BlockSpec((TOP_K, tm), lambda i: (0, i)),
                   pl.BlockSpec((TOP_K, tm), lambda i: (0, i))],
        out_shape=[jax.ShapeDtypeStruct((t, D_MODEL), F32),
                   jax.ShapeDtypeStruct((TOP_K, t), jnp.int32),
                   jax.ShapeDtypeStruct((TOP_K, t), F32)],
        compiler_params=_cparams(("parallel",)),
        name="moe_router",
    )(x, g, wr_t)


def _experts_body(te_ref, tv_ref, x_ref, sw_ref, wa_ref, wb_ref, wo_ref, o_ref, acc_ref, xb_ref):
    i, j = pl.program_id(0), pl.program_id(1)

    @pl.when(j == 0)
    def _():
        acc_ref[...] = jnp.zeros_like(acc_ref)
        xb_ref[...] = x_ref[...].astype(BF16)

    @pl.when(tv_ref[i] != 0)
    def _():
        x = xb_ref[...]
        a = jnp.dot(x, wa_ref[...], preferred_element_type=F32)
        b = jnp.dot(x, wb_ref[...], preferred_element_type=F32)
        acc_ref[...] += _bdot(a * _sigmoid(a) * b, wo_ref[...])

    @pl.when(j == pl.num_programs(1) - 1)
    def _():
        o_ref[...] = acc_ref[...] * sw_ref[...]


def _experts(tile_expert, tile_valid, xs, sw, w_in, w_out, tm, tf=512):
    p = xs.shape[0]
    nf = FF_EXPERT // tf
    grid_spec = pltpu.PrefetchScalarGridSpec(
        num_scalar_prefetch=2,
        grid=(p // tm, nf),
        in_specs=[pl.BlockSpec((tm, D_MODEL), lambda i, j, te, tv: (i, 0)),
                  pl.BlockSpec((tm, 1), lambda i, j, te, tv: (i, 0)),
                  pl.BlockSpec((None, D_MODEL, tf), lambda i, j, te, tv: (te[i], 0, j)),
                  pl.BlockSpec((None, D_MODEL, tf), lambda i, j, te, tv: (te[i], 0, j + nf)),
                  pl.BlockSpec((None, tf, D_MODEL), lambda i, j, te, tv: (te[i], j, 0))],
        out_specs=pl.BlockSpec((tm, D_MODEL), lambda i, j, te, tv: (i, 0)),
        scratch_shapes=[pltpu.VMEM((tm, D_MODEL), F32), pltpu.VMEM((tm, D_MODEL), BF16)],
    )
    return pl.pallas_call(
        _experts_body,
        grid_spec=grid_spec,
        out_shape=jax.ShapeDtypeStruct((p, D_MODEL), F32),
        compiler_params=_cparams(("parallel", "arbitrary")),
        name="moe_experts",
    )(tile_expert, tile_valid, xs, sw, w_in, w_in, w_out)


def _moe(x, g, wr_t, w_in, w_out, tm=1024):
    t = x.shape[0]
    u, idx, wts = _route(x, g, wr_t)
    e_flat = idx.reshape(-1)
    onehot = (e_flat[:, None] == jnp.arange(N_EXPERTS, dtype=jnp.int32)[None, :]).astype(jnp.int32)
    csum = jnp.cumsum(onehot, axis=0)
    rank = jnp.sum((csum - onehot) * onehot, axis=1)
    counts = csum[-1]
    padded = ((counts + tm - 1) // tm) * tm
    ends = jnp.cumsum(padded)
    starts = ends - padded
    dest = starts[e_flat] + rank
    p = TOP_K * t + N_EXPERTS * tm
    tok = jnp.tile(jnp.arange(t, dtype=jnp.int32), TOP_K)
    pair = jnp.stack([tok, lax.bitcast_convert_type(wts.reshape(-1), jnp.int32)], axis=1)
    sorted_pair = jnp.zeros((p, 2), jnp.int32).at[dest].set(pair)
    sorted_tok = sorted_pair[:, 0]
    sorted_w = lax.bitcast_convert_type(sorted_pair[:, 1], F32)
    tile_start = jnp.arange(p // tm, dtype=jnp.int32) * tm
    tile_valid = (tile_start < ends[-1]).astype(jnp.int32)
    tile_expert = jnp.minimum(jnp.sum((tile_start[:, None] >= ends[None, :]).astype(jnp.int32), axis=1),
                              N_EXPERTS - 1)
    last_e = jnp.max(jnp.where(counts > 0, jnp.arange(N_EXPERTS, dtype=jnp.int32), 0))
    tile_expert = jnp.where(tile_valid != 0, tile_expert, last_e)
    xs = jnp.take(u, sorted_tok, axis=0)
    ys = _experts(tile_expert, tile_valid, xs, sorted_w[:, None], w_in, w_out, tm)
    dest2 = dest.reshape(TOP_K, t)
    return x + jnp.take(ys, dest2[0], axis=0) + jnp.take(ys, dest2[1], axis=0)


def _norm_body(x_ref, g_ref, o_ref):
    o_ref[...] = _rms(x_ref[...], g_ref[...])


def _final_norm(x, g, tm=1024):
    t = x.shape[0]
    return pl.pallas_call(
        _norm_body,
        grid=(t // tm,),
        in_specs=[pl.BlockSpec((tm, D_MODEL), lambda i: (i, 0)), pl.BlockSpec((1, D_MODEL), lambda i: (0, 0))],
        out_specs=pl.BlockSpec((tm, D_MODEL), lambda i: (i, 0)),
        out_shape=jax.ShapeDtypeStruct((t, D_MODEL), F32),
        compiler_params=_cparams(("parallel",)),
        name="final_norm",
    )(x, g)


RET_GROUP = 4


def _ret_tables(l):
    half = RET_DK // 2
    inv_freq = ROPE_BASE ** (-jnp.arange(half, dtype=F32) / half)
    ang = jnp.arange(l, dtype=F32)[:, None] * inv_freq[None, :]
    cos, sin = jnp.cos(ang), jnp.sin(ang)
    cosf = jnp.concatenate([cos, cos], axis=1)
    sinf = jnp.concatenate([-sin, sin], axis=1)
    c = RET_CHUNK
    log_gamma = jnp.log(1.0 - jnp.exp2(-5.0 - jnp.arange(RET_HEADS, dtype=F32)))[:, None, None]
    idx = jnp.arange(c, dtype=F32)
    dmat = jnp.exp(jnp.abs(idx[:, None] - idx[None, :])[None] * log_gamma)
    col = lambda e: jnp.broadcast_to(jnp.exp(e[None, :, None] * log_gamma), (RET_HEADS, c, 128))
    sc = jnp.stack([col(idx + 1.0), col(c - idx), col(c - 1.0 - idx), col(idx),
                    jnp.broadcast_to(jnp.exp(c * log_gamma), (RET_HEADS, c, 128))], axis=1)
    return cosf, sinf, dmat, sc


def _ret_body(q_ref, k_ref, v_ref, g_ref, cos_ref, sin_ref, dm_ref, sc_ref, o_ref, qs, qd, ks, vs, uf, ub):
    c = RET_CHUNK
    n_chunks = q_ref.shape[0] // c
    q_in, q_out, k_in, k_out = sc_ref[0], sc_ref[1], sc_ref[2], sc_ref[3]
    chunk_decay = sc_ref[4]

    def rot(x, cos, sin):
        return x * cos + pltpu.roll(x, RET_DK // 2, 1) * sin

    grp = RET_GROUP

    def prep(i, carry):
        ks_f, ks_b, vs_ = [], [], []
        for j in range(grp):
            rows = pl.ds(pl.multiple_of((i * grp + j) * c, c), c)
            cos, sin = cos_ref[rows, :], sin_ref[rows, :]
            q = rot(q_ref[rows, :], cos, sin)
            k = rot(k_ref[rows, :], cos, sin) * (RET_DK ** -0.5)
            v = v_ref[rows, :].astype(BF16)
            qs[rows, :] = q.astype(BF16)
            qd[rows, :] = jnp.concatenate([q * q_in, q * q_out], axis=1).astype(BF16)
            ks[rows, :] = k.astype(BF16)
            vs[rows, :] = v
            ks_f.append((k * k_in).T.astype(BF16))
            ks_b.append((k * k_out).T.astype(BF16))
            vs_.append(v)
        for j in range(grp):
            uf[i * grp + j] = jnp.dot(ks_f[j], vs_[j], preferred_element_type=F32)
            ub[i * grp + j] = jnp.dot(ks_b[j], vs_[j], preferred_element_type=F32)
        return carry

    lax.fori_loop(0, n_chunks // grp, prep, 0)

    def fwd_state(n, s):
        u = uf[n]
        uf[n] = s
        return s * chunk_decay + u

    lax.fori_loop(0, n_chunks, fwd_state, jnp.zeros((RET_DK, RET_DV), F32))

    def bwd_state(i, s):
        n = n_chunks - 1 - i
        u = ub[n]
        ub[n] = s
        return s * chunk_decay + u

    lax.fori_loop(0, n_chunks, bwd_state, jnp.zeros((RET_DK, RET_DV), F32))

    def out(i, carry):
        rows = [pl.ds(pl.multiple_of((i * grp + j) * c, c), c) for j in range(grp)]
        ss = [_dot_nt(qs[r, :], ks[r, :]) for r in rows]
        ss = [(s * dm_ref[...]).astype(BF16) for s in ss]
        states = [jnp.concatenate([uf[i * grp + j], ub[i * grp + j]], axis=0).astype(BF16) for j in range(grp)]
        os_ = [jnp.dot(s, vs[r, :], preferred_element_type=F32) for s, r in zip(ss, rows)]
        os_ = [o + jnp.dot(qd[r, :], st, preferred_element_type=F32) for o, r, st in zip(os_, rows, states)]
        for o, r in zip(os_, rows):
            mu = jnp.mean(o, axis=-1, keepdims=True)
            d = o - mu
            o = d * lax.rsqrt(jnp.mean(d * d, axis=-1, keepdims=True) + RMS_EPS)
            g = g_ref[r, :]
            o_ref[r, :] = g * _sigmoid(g) * o
        return carry

    lax.fori_loop(0, n_chunks // grp, out, 0)


def retention(proj, b, l, col0):
    cosf, sinf, dmat, sc = _ret_tables(l)
    cb = col0 // 128
    nh = RET_HEADS
    c = RET_CHUNK
    seq = lambda off: pl.BlockSpec((l, 128), lambda i, h: (i, cb + off * nh + h))
    const2 = pl.BlockSpec((l, 128), lambda i, h: (0, 0))
    return pl.pallas_call(
        _ret_body,
        grid=(b, nh),
        in_specs=[seq(0), seq(1), seq(2), seq(3), const2, const2,
                  pl.BlockSpec((None, c, c), lambda i, h: (h, 0, 0)),
                  pl.BlockSpec((None, 5, c, 128), lambda i, h: (h, 0, 0, 0))],
        out_specs=pl.BlockSpec((l, 128), lambda i, h: (i, h)),
        out_shape=jax.ShapeDtypeStruct((b * l, nh * RET_DV), F32),
        scratch_shapes=[pltpu.VMEM((l, 128), BF16), pltpu.VMEM((l, 256), BF16), pltpu.VMEM((l, 128), BF16),
                        pltpu.VMEM((l, 128), BF16), pltpu.VMEM((l // c, 128, 128), F32),
                        pltpu.VMEM((l // c, 128, 128), F32)],
        compiler_params=_cparams(("parallel", "parallel")),
        name="retention",
    )(proj, proj, proj, proj, cosf, sinf, dmat, sc)


GDN_ROWS = 512
HALO = 8


def _cumsum_rows(x, reverse):
    n = x.shape[0]
    row = lax.broadcasted_iota(jnp.int32, x.shape, 0)
    s = 1
    while s < n:
        if reverse:
            x = x + jnp.where(row < n - s, pltpu.roll(x, n - s, 0), 0.0)
        else:
            x = x + jnp.where(row >= s, pltpu.roll(x, s, 0), 0.0)
        s *= 2
    return x


def _gdn_local(insts, neg_a, dtb):
    c = GDN_CHUNK
    row = lax.broadcasted_iota(jnp.int32, (c, c), 0)
    col = lax.broadcasted_iota(jnp.int32, (c, c), 1)
    pre = []
    for q, k, v, graw, braw, d in insts:
        reverse = d == 1
        x = graw + dtb[d]
        softplus = jnp.maximum(x, 0.0) + jnp.log(1.0 + jnp.exp(-jnp.abs(x)))
        log_g = neg_a[d] * softplus
        beta = _sigmoid(braw)
        cum = _cumsum_rows(jnp.broadcast_to(log_g, (c, 128)), reverse)
        total = cum[0:1, :] if reverse else cum[c - 1:c, :]
        cum_sq = cum[:, :c]
        cum_row = jnp.sum(jnp.where(row == col, cum_sq, 0.0), axis=0, keepdims=True)
        incl = (row <= col) if reverse else (row >= col)
        strict = (row < col) if reverse else (row > col)
        decay = jnp.where(incl, jnp.exp(jnp.where(incl, cum_sq - cum_row, 0.0)), 0.0)
        pre.append((cum, total, decay, strict, k * beta, v * beta))
    qks = [_dot_nt(jnp.concatenate([q, p[4]], axis=0), k) for (q, k, *_), p in zip(insts, pre)]
    bps = [-jnp.where(p[3], qk[c:] * p[2], 0.0) for qk, p in zip(qks, pre)]
    rs = list(bps)
    for _ in range(5):
        bps = [_dot(bp, bp) for bp in bps]
        rs = [r + bp + _dot(r, bp) for r, bp in zip(rs, bps)]
    rhss = [jnp.concatenate([p[5], p[4] * jnp.exp(p[0])], axis=1) for p in pre]
    sols = [rhs + _dot(r, rhs) for r, rhs in zip(rs, rhss)]
    kdts, attns = [], []
    for (q, k, *_), p, qk in zip(insts, pre, qks):
        cum, total, decay = p[0], p[1], p[2]
        k_dec = k * jnp.exp(total - cum)
        kdts.append(jnp.concatenate([k_dec, jnp.zeros((c, 128), F32)], axis=0).T[:, :c])
        attns.append(qk[:c] * decay)
    kus = [_dot(kd_t, sol) for kd_t, sol in zip(kdts, sols)]
    aos = [_dot(attn, sol) for attn, sol in zip(attns, sols)]
    outs = []
    for (q, *_), p, ku, ao in zip(insts, pre, kus, aos):
        q_eff = q * jnp.exp(p[0]) - ao[:, GDN_DV:]
        outs.append((jnp.concatenate([ku[:, GDN_DV:], q_eff], axis=0), ku[:, :GDN_DV], ao[:, :GDN_DV], jnp.exp(p[1])))
    return outs


def _gdn_body(alog_ref, dtb_ref, q_ref, k_ref, v_ref, gate_ref, gab_ref, cwq_ref, cwk_ref, cwv_ref, onorm_ref,
              o_ref, pad, km_s, u_s, o0_s, gl_s, sf, sb):
    c = GDN_CHUNK
    l = q_ref.shape[0]
    h = pl.program_id(1)
    n_chunks = l // c
    per_block = GDN_ROWS // c

    zero_halo = jnp.zeros((HALO, 128), F32)
    for a, src in enumerate((q_ref, k_ref, v_ref)):
        pad[a, 0:HALO, :] = zero_halo
        pad[a, l + HALO:l + 2 * HALO, :] = zero_halo
        pad[a, HALO:l + HALO, :] = src[...]

    ones = jnp.ones((c, 1), F32)
    neg_a = (-jnp.exp(ones * alog_ref[0, h]), -jnp.exp(ones * alog_ref[1, h]))
    dtb = (dtb_ref[0, h], dtb_ref[1, h])
    win_rows = GDN_ROWS + 2 * HALO

    def local_block(i, carry):
        s0 = pl.multiple_of(i * GDN_ROWS, GDN_ROWS)
        ys = []
        for a, cw_ref in enumerate((cwq_ref, cwk_ref, cwv_ref)):
            win = pad[a, pl.ds(s0, win_rows), :]
            acc = cw_ref[2:3, :] * win
            for j in (0, 1, 3, 4):
                acc = acc + cw_ref[j:j + 1, :] * pltpu.roll(win, (GDN_CONV // 2 - j) % win_rows, 0)
            y = acc[HALO:HALO + GDN_ROWS]
            y = y * _sigmoid(y)
            if a < 2:
                y = y * lax.rsqrt(jnp.sum(y * y, axis=-1, keepdims=True) + 1e-6)
            if a == 0:
                y = y * (GDN_DK ** -0.5)
            ys.append(y)
        ab = gab_ref[pl.ds(s0, GDN_ROWS), :]
        insts = []
        for j in range(per_block):
            sl = slice(j * c, (j + 1) * c)
            for d in range(2):
                insts.append((ys[0][sl], ys[1][sl], ys[2][sl], ab[sl, d:d + 1], ab[sl, 2 + d:3 + d], d))
        outs = _gdn_local(insts, neg_a, dtb)
        for idx, (km, u, o0, g_last) in enumerate(outs):
            m = (idx % 2) * n_chunks + i * per_block + idx // 2
            km_s[m] = km.astype(BF16)
            u_s[m] = u.astype(BF16)
            o0_s[m] = o0
            gl_s[m] = jnp.broadcast_to(g_last, (8, 128))
        return carry

    lax.fori_loop(0, l // GDN_ROWS, local_block, 0)

    sf[...] = jnp.zeros_like(sf)
    sb[...] = jnp.zeros_like(sb)

    def scan(i, carry):
        for d, s_ref in ((0, sf), (1, sb)):
            n = i if d == 0 else n_chunks - 1 - i
            m = d * n_chunks + n
            rows = pl.ds(pl.multiple_of(n * c, c) + HALO, c)
            s = s_ref[...]
            r = jnp.dot(km_s[m], s.astype(BF16), preferred_element_type=F32)
            pad[d, rows, :] = o0_s[m] + r[GDN_DK:]
            s_ref[...] = s * gl_s[m][0:1, :] - r[:GDN_DK] + u_s[m].astype(F32)
        return carry

    lax.fori_loop(0, n_chunks, scan, 0)

    def fin(i, carry):
        r0 = pl.multiple_of(i * GDN_ROWS, GDN_ROWS)
        o = pad[0, pl.ds(r0 + HALO, GDN_ROWS), :] + pad[1, pl.ds(r0 + HALO, GDN_ROWS), :]
        o = o * lax.rsqrt(jnp.mean(o * o, axis=-1, keepdims=True) + RMS_EPS) * onorm_ref[...]
        g = gate_ref[pl.ds(r0, GDN_ROWS), :]
        o_ref[pl.ds(r0, GDN_ROWS), :] = o * (g * _sigmoid(g))
        return carry

    lax.fori_loop(0, l // GDN_ROWS, fin, 0)


def gdn(proj, gabh, conv_w, a_log, dt_bias, onorm, b, l, col0):
    cb = col0 // 128
    nh = GDN_HEADS
    nc = l // GDN_CHUNK
    seq = lambda off: pl.BlockSpec((l, 128), lambda i, h, *_: (i, cb + off * nh + h))
    cw = lambda off: pl.BlockSpec((GDN_CONV, 128), lambda i, h, *_: (0, off * nh + h))
    grid_spec = pltpu.PrefetchScalarGridSpec(
        num_scalar_prefetch=2,
        grid=(b, nh),
        in_specs=[seq(0), seq(1), seq(2), seq(3),
                  pl.BlockSpec((None, l, 4), lambda i, h, *_: (h, i, 0)),
                  cw(0), cw(1), cw(2),
                  pl.BlockSpec((1, 128), lambda i, h, *_: (0, 0))],
        out_specs=pl.BlockSpec((l, 128), lambda i, h, *_: (i, h)),
        scratch_shapes=[pltpu.VMEM((3, l + 2 * HALO, 128), F32),
                        pltpu.VMEM((2 * nc, GDN_DK + GDN_CHUNK, GDN_DK), BF16),
                        pltpu.VMEM((2 * nc, GDN_DK, GDN_DV), BF16),
                        pltpu.VMEM((2 * nc, GDN_CHUNK, GDN_DV), F32),
                        pltpu.VMEM((2 * nc, 8, 128), F32),
                        pltpu.VMEM((GDN_DK, GDN_DV), F32), pltpu.VMEM((GDN_DK, GDN_DV), F32)],
    )
    return pl.pallas_call(
        _gdn_body,
        grid_spec=grid_spec,
        out_shape=jax.ShapeDtypeStruct((b * l, nh * GDN_DV), F32),
        compiler_params=_cparams(("parallel", "parallel")),
        name="gated_deltanet",
    )(a_log, dt_bias, proj, proj, proj, proj, gabh, conv_w, conv_w, conv_w, onorm[None, :])


HY_N1 = 64
HY_ROWS = 256


def _hyconv_body(x0_ref, x1_ref, v_ref, c0_ref, c1_ref, cv_ref, z_ref, x0c_ref, pad):
    l = x0_ref.shape[0]
    zero_halo = jnp.zeros((HALO, 128), F32)
    for a, src in enumerate((x0_ref, x1_ref, v_ref)):
        pad[a, 0:HALO, :] = zero_halo
        pad[a, l + HALO:l + 2 * HALO, :] = zero_halo
        pad[a, HALO:l + HALO, :] = src[...]
    win_rows = HY_ROWS + 2 * HALO

    def block(i, carry):
        s0 = pl.multiple_of(i * HY_ROWS, HY_ROWS)
        outs = []
        for a, cw_ref in enumerate((c0_ref, c1_ref, cv_ref)):
            win = pad[a, pl.ds(s0, win_rows), :]
            acc = cw_ref[1:2, :] * win
            acc = acc + cw_ref[0:1, :] * pltpu.roll(win, 1, 0)
            acc = acc + cw_ref[2:3, :] * pltpu.roll(win, win_rows - 1, 0)
            outs.append(acc[HALO:HALO + HY_ROWS])
        rows = pl.ds(s0, HY_ROWS)
        x0c_ref[rows, :] = outs[0]
        z_ref[rows, :] = outs[2] * outs[1]
        return carry

    lax.fori_loop(0, l // HY_ROWS, block, 0)


def _hy_conv(proj, conv_w, b, l, col0):
    cb = col0 // 128
    nc = HY_WIDTH // 128
    seq = lambda off: pl.BlockSpec((l, 128), lambda i, c: (i, cb + off * nc + c))
    cw = lambda off: pl.BlockSpec((HY_SHORT, 128), lambda i, c: (0, off * nc + c))
    out = pl.BlockSpec((l, 128), lambda i, c: (i, c))
    return pl.pallas_call(
        _hyconv_body,
        grid=(b, nc),
        in_specs=[seq(0), seq(1), seq(2), cw(0), cw(1), cw(2)],
        out_specs=[out, out],
        out_shape=[jax.ShapeDtypeStruct((b * l, HY_WIDTH), F32), jax.ShapeDtypeStruct((b * l, HY_WIDTH), F32)],
        scratch_shapes=[pltpu.VMEM((3, l + 2 * HALO, 128), F32)],
        compiler_params=_cparams(("parallel", "parallel")),
        name="hyena_short_conv",
    )(proj, proj, proj, conv_w, conv_w, conv_w)


def _hyfilt_body(feat_ref, w1_ref, b1_ref, w2_ref, b2_ref, w3_ref, b3_ref, fr_ref, wo_ref, dec_ref, o_ref, *, l):
    tm = feat_ref.shape[0]
    hdot = lambda a, b_: jnp.dot(a, b_, precision=HI, preferred_element_type=F32)
    feat = feat_ref[...]
    fr = fr_ref[...]
    h = jnp.sin(fr * (hdot(feat, w1_ref[...]) + b1_ref[...]))
    h = jnp.sin(fr * (hdot(h, w2_ref[...]) + b2_ref[...]))
    h = jnp.sin(fr * (hdot(h, w3_ref[...]) + b3_ref[...]))
    h = hdot(h, wo_ref[...]) * jnp.exp(-feat[:, 0:1] * jnp.abs(dec_ref[...]))
    m = pl.program_id(0) * tm + lax.broadcasted_iota(jnp.int32, (tm, 1), 0)
    o_ref[...] = jnp.where(m == l, 0.0, h)


def _hy_filter(l, w1, b1, w2, b2, w3, b3, freq, wout, decay, tm=512):
    n = 2 * l
    m = jnp.arange(n, dtype=jnp.int32)
    t = jnp.where(m < l, m, n - m).astype(F32)[:, None]
    t01 = t / (l - 1)
    bands = (HY_EMB - 1) // 2
    f = jnp.linspace(1e-4, bands - 1, bands, dtype=F32)[None, :]
    w = 2.0 * math.pi * t / l
    feat = jnp.concatenate([t01, jnp.cos(f * w), -jnp.sin(f * w)], axis=-1)
    const = lambda i: (0, 0)
    half = lambda i: (0, (i * tm) // l)
    vec = lambda x: x[None, :]
    return pl.pallas_call(
        functools.partial(_hyfilt_body, l=l),
        grid=(n // tm,),
        in_specs=[pl.BlockSpec((tm, HY_EMB), lambda i: (i, 0)),
                  pl.BlockSpec((HY_EMB, HY_FFN), const), pl.BlockSpec((1, HY_FFN), const),
                  pl.BlockSpec((HY_FFN, HY_FFN), const), pl.BlockSpec((1, HY_FFN), const),
                  pl.BlockSpec((HY_FFN, HY_FFN), const), pl.BlockSpec((1, HY_FFN), const),
                  pl.BlockSpec((1, HY_FFN), const),
                  pl.BlockSpec((HY_FFN, HY_WIDTH), half),
                  pl.BlockSpec((None, 1, HY_WIDTH), lambda i: ((i * tm) // l, 0, 0))],
        out_specs=pl.BlockSpec((tm, HY_WIDTH), lambda i: (i, 0)),
        out_shape=jax.ShapeDtypeStruct((n, HY_WIDTH), F32),
        compiler_params=_cparams(("parallel",)),
        name="hyena_filter",
    )(feat, w1, vec(b1), w2, vec(b2), w3, vec(b3), vec(freq), wout, decay[:, None, :])


def _dft_tables(l):
    n = 2 * l
    n1, n2 = HY_N1, n // HY_N1
    ang = lambda num, den: (2.0 * math.pi / den) * (num % den).astype(F32)
    i1 = jnp.arange(n1, dtype=jnp.int32)
    a1 = ang(i1[:, None] * i1[None, :], n1)
    fwd1 = jnp.concatenate([jnp.cos(a1), -jnp.sin(a1)], axis=0)
    a1h = a1[:, :n1 // 2].T
    inv1 = jnp.concatenate([jnp.cos(a1h), -jnp.sin(a1h)], axis=1) / n
    i2 = jnp.arange(n2, dtype=jnp.int32)
    freq = i1[:, None, None] + n1 * i2[None, :, None]
    a2 = ang(freq * i2[None, None, :], n)
    gr, gi = jnp.cos(a2), -jnp.sin(a2)
    fwd2 = jnp.concatenate([jnp.concatenate([gr, -gi], axis=2), jnp.concatenate([gi, gr], axis=2)], axis=1)
    inv2 = jnp.swapaxes(fwd2, 1, 2)
    return fwd1, inv1, fwd2, inv2


HY_TR = 8


def _dft1_body(x_ref, f_ref, o_ref, *, exact):
    k1, tr, c = x_ref.shape
    x = x_ref[...].reshape(k1 * tr, c)
    if exact:
        o = jnp.dot(f_ref[...], x, precision=HI, preferred_element_type=F32)
    else:
        o = jnp.dot(f_ref[...], x.astype(BF16), preferred_element_type=F32)
    o_ref[...] = o.astype(o_ref.dtype).reshape(o_ref.shape)


def _dft1(x4, mat, out_dtype, exact):
    bt, k1, n2, c = x4.shape
    n1 = mat.shape[0] // 2
    kron = jnp.kron(mat, jnp.eye(HY_TR, dtype=mat.dtype))
    return pl.pallas_call(
        functools.partial(_dft1_body, exact=exact),
        grid=(bt, n2 // HY_TR),
        in_specs=[pl.BlockSpec((None, k1, HY_TR, c), lambda i, j: (i, 0, j, 0)),
                  pl.BlockSpec((2 * n1 * HY_TR, k1 * HY_TR), lambda i, j: (0, 0))],
        out_specs=pl.BlockSpec((None, 2, n1, HY_TR, c), lambda i, j: (i, 0, 0, j, 0)),
        out_shape=jax.ShapeDtypeStruct((bt, 2, n1, n2, c), out_dtype),
        compiler_params=_cparams(("parallel", "parallel")),
        name="hyena_dft_stage1",
    )(x4, kron)


def _spec2_body(a_ref, g_ref, o_ref):
    n2 = a_ref.shape[1]
    a = a_ref[...].reshape(2 * n2, a_ref.shape[2])
    o = jnp.dot(g_ref[...], a, precision=HI, preferred_element_type=F32)
    o_ref[...] = o.reshape(o_ref.shape)


def _filter_spectrum(circ, fwd1, fwd2, l):
    n1, n2 = HY_N1, 2 * l // HY_N1
    a5 = _dft1(circ.reshape(1, n1, n2, HY_WIDTH), fwd1, F32, True)[0]
    return pl.pallas_call(
        _spec2_body,
        grid=(n1,),
        in_specs=[pl.BlockSpec((2, None, n2, HY_WIDTH), lambda i: (0, i, 0, 0)),
                  pl.BlockSpec((None, 2 * n2, 2 * n2), lambda i: (i, 0, 0))],
        out_specs=pl.BlockSpec((None, 2, n2, HY_WIDTH), lambda i: (i, 0, 0, 0)),
        out_shape=jax.ShapeDtypeStruct((n1, 2, n2, HY_WIDTH), F32),
        compiler_params=_cparams(("parallel",)),
        name="hyena_filter_spectrum",
    )(a5, fwd2)


def _hymid_body(a_ref, g_ref, gi_ref, h_ref, o_ref):
    bb, _, n2, c = a_ref.shape
    a = jnp.concatenate([a_ref[i].reshape(2 * n2, c) for i in range(bb)], axis=1)
    x = jnp.dot(g_ref[...], a, preferred_element_type=F32)
    hr = jnp.concatenate([h_ref[0]] * bb, axis=1)
    hi = jnp.concatenate([h_ref[1]] * bb, axis=1)
    xr, xi = x[:n2], x[n2:]
    y = jnp.concatenate([xr * hr - xi * hi, xr * hi + xi * hr], axis=0).astype(BF16)
    o = jnp.dot(gi_ref[...], y, preferred_element_type=F32).astype(o_ref.dtype)
    for i in range(bb):
        o_ref[i] = o[:, i * c:(i + 1) * c].reshape(2, n2, c)


def _hy_mid(a5, fwd2, inv2, spec, bb=4):
    b, _, n1, n2, c = a5.shape
    blk = pl.BlockSpec((bb, 2, None, n2, c), lambda f, i: (i, 0, f, 0, 0))
    mat = pl.BlockSpec((None, 2 * n2, 2 * n2), lambda f, i: (f, 0, 0))
    return pl.pallas_call(
        _hymid_body,
        grid=(n1, b // bb),
        in_specs=[blk, mat, mat, pl.BlockSpec((None, 2, n2, c), lambda f, i: (f, 0, 0, 0))],
        out_specs=blk,
        out_shape=jax.ShapeDtypeStruct(a5.shape, BF16),
        compiler_params=_cparams(("parallel", "parallel")),
        name="hyena_spectral_product",
    )(a5, fwd2, inv2, spec)


def _hyfin_body(b_ref, m_ref, z_ref, x0_ref, bias_ref, o_ref):
    k1, tr, c = z_ref.shape
    y = jnp.dot(m_ref[...], b_ref[...].reshape(m_ref.shape[1], c), preferred_element_type=F32).reshape(k1, tr, c)
    o_ref[...] = (y + z_ref[...] * bias_ref[...]) * x0_ref[...]


def _hy_final(bq5, inv1, z4, x04, bias):
    b, _, n1, n2, c = bq5.shape
    k1 = z4.shape[1]
    kron = jnp.kron(inv1, jnp.eye(HY_TR, dtype=inv1.dtype))
    row = pl.BlockSpec((None, k1, HY_TR, c), lambda i, j: (i, 0, j, 0))
    return pl.pallas_call(
        _hyfin_body,
        grid=(b, n2 // HY_TR),
        in_specs=[pl.BlockSpec((None, 2, n1, HY_TR, c), lambda i, j: (i, 0, 0, j, 0)),
                  pl.BlockSpec((k1 * HY_TR, 2 * n1 * HY_TR), lambda i, j: (0, 0)), row, row,
                  pl.BlockSpec((1, 1, c), lambda i, j: (0, 0, 0))],
        out_specs=row,
        out_shape=jax.ShapeDtypeStruct(z4.shape, F32),
        compiler_params=_cparams(("parallel", "parallel")),
        name="hyena_dft_final",
    )(bq5, kron, z4, x04, bias[None, None, :])


def hyena(proj, conv_w, w1, b1, w2, b2, w3, b3, freq, wout, decay, bias, b, l, col0):
    n1, n2, c = HY_N1, 2 * l // HY_N1, HY_WIDTH
    fwd1, inv1, fwd2, inv2 = _dft_tables(l)
    circ = _hy_filter(l, w1, b1, w2, b2, w3, b3, freq, wout, decay)
    spec = _filter_spectrum(circ, fwd1, fwd2, l)
    z, x0c = _hy_conv(proj, conv_w, b, l, col0)
    z4 = z.reshape(b, n1 // 2, n2, c)
    a5 = _dft1(z4, fwd1[:, :n1 // 2].astype(BF16), BF16, False)
    bq5 = _hy_mid(a5, fwd2.astype(BF16), inv2.astype(BF16), spec)
    out = _hy_final(bq5, inv1.astype(BF16), z4, x0c.reshape(b, n1 // 2, n2, c), bias)
    return out.reshape(b * l, c)


def _reorder_w_in(w):
    rest, gab, hy, gates = w[:, :4096], w[:, 4096:4112], w[:, 4112:5648], w[:, 5648:]
    pad = jnp.zeros((D_MODEL, D_PROJ - COL_GAB - 16), w.dtype)
    return jnp.concatenate([gates, rest, hy, gab, pad], axis=1).astype(BF16)


def _trunk(x3, p):
    b, l, _ = x3.shape
    t = b * l
    x = x3.reshape(t, D_MODEL)
    for i in range(DEPTH):
        proj = _proj(x, p['norm_mix'][i][None, :], p['w_in_r'][i])
        y_ret = retention(proj, b, l, COL_RET)
        gabh = jnp.transpose(proj[:, COL_GAB:COL_GAB + 4 * GDN_HEADS].reshape(t, 4, GDN_HEADS), (2, 0, 1))
        y_gdn = gdn(proj, gabh, p['gdn_conv_w'][i], p['gdn_A_log'][i], p['gdn_dt_bias'][i], p['gdn_onorm'][i],
                    b, l, COL_GDN)
        y_hy = hyena(proj, p['hy_conv_w'][i], p['hy_w1'][i], p['hy_b1'][i], p['hy_w2'][i], p['hy_b2'][i],
                     p['hy_w3'][i], p['hy_b3'][i], p['hy_freq'][i], p['hy_wout'][i], p['hy_decay'][i],
                     p['hy_bias'][i], b, l, COL_HY)
        x = _merge(x, proj, y_ret, y_gdn, y_hy, p['w_ret_o_b'][i], p['w_gdn_o_b'][i], p['w_hy_o_b'][i], p['w_o_b'][i])
        g = p['norm_ffn'][i][None, :]
        if i % 2 == 0:
            x = _ffn(x, g, p['dense_w_in_b'][i // 2], p['dense_w_out_b'][i // 2])
        else:
            x = _moe(x, g, p['moe_router_t'][i // 2], p['moe_w_in_b'][i // 2], p['moe_w_out_b'][i // 2])
    return _final_norm(x, p['norm_final'][None, :]).reshape(b, l, D_MODEL)


def kernel(x_prompt, x_sample, norm_mix, w_in, gdn_conv_w, gdn_A_log, gdn_dt_bias, gdn_onorm, hy_conv_w, hy_w1, hy_b1, hy_w2, hy_b2, hy_w3, hy_b3, hy_freq, hy_wout, hy_decay, hy_bias, w_ret_o, w_gdn_o, w_hy_o, w_o, norm_ffn, dense_w_in, dense_w_out, moe_router, moe_w_in, moe_w_out, norm_final):
    p = dict(norm_mix=norm_mix, gdn_conv_w=gdn_conv_w, gdn_A_log=gdn_A_log, gdn_dt_bias=gdn_dt_bias,
             gdn_onorm=gdn_onorm, hy_conv_w=hy_conv_w, hy_w1=hy_w1, hy_b1=hy_b1, hy_w2=hy_w2, hy_b2=hy_b2,
             hy_w3=hy_w3, hy_b3=hy_b3, hy_freq=hy_freq, hy_wout=hy_wout, hy_decay=hy_decay, hy_bias=hy_bias,
             norm_ffn=norm_ffn, norm_final=norm_final)
    p['w_in_r'] = jax.vmap(_reorder_w_in)(w_in)
    p['w_ret_o_b'] = w_ret_o.astype(BF16)
    p['w_gdn_o_b'] = w_gdn_o.astype(BF16)
    p['w_hy_o_b'] = w_hy_o.astype(BF16)
    p['w_o_b'] = w_o.astype(BF16)
    p['dense_w_in_b'] = dense_w_in.astype(BF16)
    p['dense_w_out_b'] = dense_w_out.astype(BF16)
    p['moe_router_t'] = jnp.swapaxes(moe_router, 1, 2)
    p['moe_w_in_b'] = moe_w_in.astype(BF16)
    p['moe_w_out_b'] = moe_w_out.astype(BF16)
    return (_trunk(x_prompt, p), _trunk(x_sample, p))
```

```python
import functools
import math

import jax
import jax.numpy as jnp
from jax import lax
from jax.experimental import pallas as pl
from jax.experimental.pallas import tpu as pltpu

F32 = jnp.float32
BF16 = jnp.bfloat16

D_MODEL = 1024
DEPTH = 2
RMS_EPS = 1e-6
RET_HEADS, RET_DK, RET_DV, RET_CHUNK = 4, 128, 128, 128
ROPE_BASE = 10000.0
GDN_HEADS, GDN_DK, GDN_DV, GDN_CHUNK, GDN_CONV = 4, 128, 128, 64, 5
HY_WIDTH, HY_SHORT, HY_EMB, HY_FFN = 512, 3, 33, 64
FF_DENSE = 2816
N_EXPERTS, TOP_K, FF_EXPERT = 8, 2, 3584

COL_GATES = 0
COL_RET = 3072
COL_GDN = 5120
COL_HY = 7168
COL_GAB = 8704
D_PROJ = 8832

VMEM_LIMIT = 56 * 1024 * 1024
HI = lax.Precision.HIGHEST


def _cparams(sem):
    return pltpu.CompilerParams(dimension_semantics=sem, vmem_limit_bytes=VMEM_LIMIT)


def _rms(x, g):
    return x * lax.rsqrt(jnp.mean(x * x, axis=-1, keepdims=True) + RMS_EPS) * g


def _sigmoid(x):
    return 1.0 / (1.0 + jnp.exp(-x))


def _bdot(a, b):
    return jnp.dot(a.astype(BF16), b, preferred_element_type=F32)


def _dot(a, b):
    return jnp.dot(a.astype(BF16), b.astype(BF16), preferred_element_type=F32)


def _dot_nt(a, b):
    return lax.dot_general(a.astype(BF16), b.astype(BF16), (((1,), (1,)), ((), ())), preferred_element_type=F32)


def _proj_body(x_ref, g_ref, w_ref, o_ref, u_ref):
    @pl.when(pl.program_id(1) == 0)
    def _():
        u_ref[...] = _rms(x_ref[...], g_ref[...]).astype(BF16)

    o_ref[...] = jnp.dot(u_ref[...], w_ref[...], preferred_element_type=F32)


def _proj(x, g, w, tm=1024, tn=2944):
    t = x.shape[0]
    return pl.pallas_call(
        _proj_body,
        grid=(t // tm, D_PROJ // tn),
        in_specs=[pl.BlockSpec((tm, D_MODEL), lambda i, j: (i, 0)),
                  pl.BlockSpec((1, D_MODEL), lambda i, j: (0, 0)),
                  pl.BlockSpec((D_MODEL, tn), lambda i, j: (0, j))],
        out_specs=pl.BlockSpec((tm, tn), lambda i, j: (i, j)),
        out_shape=jax.ShapeDtypeStruct((t, D_PROJ), F32),
        scratch_shapes=[pltpu.VMEM((tm, D_MODEL), BF16)],
        compiler_params=_cparams(("parallel", "arbitrary")),
        name="in_proj",
    )(x, g, w)


def _merge_body(x_ref, yr_ref, yg_ref, yh_ref, gr_ref, gg_ref, gh_ref, wr_ref, wg_ref, wh_ref, wo_ref, o_ref):
    merged = (_sigmoid(gr_ref[...]) * _bdot(yr_ref[...], wr_ref[...])
              + _sigmoid(gg_ref[...]) * _bdot(yg_ref[...], wg_ref[...])
              + _sigmoid(gh_ref[...]) * _bdot(yh_ref[...], wh_ref[...]))
    o_ref[...] = x_ref[...] + _bdot(merged, wo_ref[...])


def _merge(x, proj, y_ret, y_gdn, y_hy, w_ret, w_gdn, w_hy, w_o, tm=512):
    t = x.shape[0]
    row = lambda i: (i, 0)
    const = lambda i: (0, 0)
    gate = lambda b: pl.BlockSpec((tm, D_MODEL), lambda i: (i, COL_GATES // D_MODEL + b))
    return pl.pallas_call(
        _merge_body,
        grid=(t // tm,),
        in_specs=[pl.BlockSpec((tm, D_MODEL), row),
                  pl.BlockSpec((tm, 512), row), pl.BlockSpec((tm, 512), row), pl.BlockSpec((tm, 512), row),
                  gate(0), gate(1), gate(2),
                  pl.BlockSpec((512, D_MODEL), const), pl.BlockSpec((512, D_MODEL), const),
                  pl.BlockSpec((512, D_MODEL), const), pl.BlockSpec((D_MODEL, D_MODEL), const)],
        out_specs=pl.BlockSpec((tm, D_MODEL), row),
        out_shape=jax.ShapeDtypeStruct((t, D_MODEL), F32),
        compiler_params=_cparams(("parallel",)),
        name="merge",
    )(x, y_ret, y_gdn, y_hy, proj, proj, proj, w_ret, w_gdn, w_hy, w_o)


def _ffn_body(x_ref, g_ref, wa_ref, wb_ref, wo_ref, o_ref, u_ref, acc_ref):
    j = pl.program_id(1)

    @pl.when(j == 0)
    def _():
        u_ref[...] = _rms(x_ref[...], g_ref[...]).astype(BF16)
        acc_ref[...] = jnp.zeros_like(acc_ref)

    u = u_ref[...]
    a = jnp.dot(u, wa_ref[...], preferred_element_type=F32)
    b = jnp.dot(u, wb_ref[...], preferred_element_type=F32)
    acc_ref[...] += _bdot(a * _sigmoid(a) * b, wo_ref[...])

    @pl.when(j == pl.num_programs(1) - 1)
    def _():
        o_ref[...] = x_ref[...] + acc_ref[...]


def _ffn(x, g, w_in, w_out, tm=512, tf=1408):
    t = x.shape[0]
    nf = FF_DENSE // tf
    return pl.pallas_call(
        _ffn_body,
        grid=(t // tm, nf),
        in_specs=[pl.BlockSpec((tm, D_MODEL), lambda i, j: (i, 0)),
                  pl.BlockSpec((1, D_MODEL), lambda i, j: (0, 0)),
                  pl.BlockSpec((D_MODEL, tf), lambda i, j: (0, j)),
                  pl.BlockSpec((D_MODEL, tf), lambda i, j: (0, j + nf)),
                  pl.BlockSpec((tf, D_MODEL), lambda i, j: (j, 0))],
        out_specs=pl.BlockSpec((tm, D_MODEL), lambda i, j: (i, 0)),
        out_shape=jax.ShapeDtypeStruct((t, D_MODEL), F32),
        scratch_shapes=[pltpu.VMEM((tm, D_MODEL), BF16), pltpu.VMEM((tm, D_MODEL), F32)],
        compiler_params=_cparams(("parallel", "arbitrary")),
        name="dense_ffn",
    )(x, g, w_in, w_in, w_out)


def _route_body(x_ref, g_ref, wr_ref, u_ref, idx_ref, wt_ref):
    u = _rms(x_ref[...], g_ref[...])
    u_ref[...] = u
    logits = lax.dot_general(wr_ref[...], u, (((1,), (1,)), ((), ())), precision=HI, preferred_element_type=F32)
    row = lax.broadcasted_iota(jnp.int32, logits.shape, 0)
    m1 = jnp.max(logits, axis=0, keepdims=True)
    i1 = jnp.min(jnp.where(logits == m1, row, N_EXPERTS), axis=0, keepdims=True)
    rest = jnp.where(row == i1, -jnp.inf, logits)
    m2 = jnp.max(rest, axis=0, keepdims=True)
    i2 = jnp.min(jnp.where(rest == m2, row, N_EXPERTS), axis=0, keepdims=True)
    e = jnp.exp(m2 - m1)
    idx_ref[0:1, :] = i1
    idx_ref[1:2, :] = i2
    wt_ref[0:1, :] = 1.0 / (1.0 + e)
    wt_ref[1:2, :] = e / (1.0 + e)


def _route(x, g, wr_t, tm=512):
    t = x.shape[0]
    return pl.pallas_call(
        _route_body,
        grid=(t // tm,),
        in_specs=[pl.BlockSpec((tm, D_MODEL), lambda i: (i, 0)),
                  pl.BlockSpec((1, D_MODEL), lambda i: (0, 0)),
                  pl.BlockSpec((N_EXPERTS, D_MODEL), lambda i: (0, 0))],
        out_specs=[pl.BlockSpec((tm, D_MODEL), lambda i: (i, 0)),
                   pl.BlockSpec((TOP_K, tm), lambda i: (0, i)),
                   pl.BlockSpec((TOP_K, tm), lambda i: (0, i))],
        out_shape=[jax.ShapeDtypeStruct((t, D_MODEL), F32),
                   jax.ShapeDtypeStruct((TOP_K, t), jnp.int32),
                   jax.ShapeDtypeStruct((TOP_K, t), F32)],
        compiler_params=_cparams(("parallel",)),
        name="moe_router",
    )(x, g, wr_t)


def _experts_body(te_ref, tv_ref, x_ref, sw_ref, wa_ref, wb_ref, wo_ref, o_ref, acc_ref, xb_ref):
    i, j = pl.program_id(0), pl.program_id(1)

    @pl.when(j == 0)
    def _():
        acc_ref[...] = jnp.zeros_like(acc_ref)
        xb_ref[...] = x_ref[...].astype(BF16)

    @pl.when(tv_ref[i] != 0)
    def _():
        x = xb_ref[...]
        a = jnp.dot(x, wa_ref[...], preferred_element_type=F32)
        b = jnp.dot(x, wb_ref[...], preferred_element_type=F32)
        acc_ref[...] += _bdot(a * _sigmoid(a) * b, wo_ref[...])

    @pl.when(j == pl.num_programs(1) - 1)
    def _():
        o_ref[...] = acc_ref[...] * sw_ref[...]


def _experts(tile_expert, tile_valid, xs, sw, w_in, w_out, tm, tf=512):
    p = xs.shape[0]
    nf = FF_EXPERT // tf
    grid_spec = pltpu.PrefetchScalarGridSpec(
        num_scalar_prefetch=2,
        grid=(p // tm, nf),
        in_specs=[pl.BlockSpec((tm, D_MODEL), lambda i, j, te, tv: (i, 0)),
                  pl.BlockSpec((tm, 1), lambda i, j, te, tv: (i, 0)),
                  pl.BlockSpec((None, D_MODEL, tf), lambda i, j, te, tv: (te[i], 0, j)),
                  pl.BlockSpec((None, D_MODEL, tf), lambda i, j, te, tv: (te[i], 0, j + nf)),
                  pl.BlockSpec((None, tf, D_MODEL), lambda i, j, te, tv: (te[i], j, 0))],
        out_specs=pl.BlockSpec((tm, D_MODEL), lambda i, j, te, tv: (i, 0)),
        scratch_shapes=[pltpu.VMEM((tm, D_MODEL), F32), pltpu.VMEM((tm, D_MODEL), BF16)],
    )
    return pl.pallas_call(
        _experts_body,
        grid_spec=grid_spec,
        out_shape=jax.ShapeDtypeStruct((p, D_MODEL), F32),
        compiler_params=_cparams(("parallel", "arbitrary")),
        name="moe_experts",
    )(tile_expert, tile_valid, xs, sw, w_in, w_in, w_out)


def _moe(x, g, wr_t, w_in, w_out, tm=1024):
    t = x.shape[0]
    u, idx, wts = _route(x, g, wr_t)
    e_flat = idx.reshape(-1)
    onehot = (e_flat[:, None] == jnp.arange(N_EXPERTS, dtype=jnp.int32)[None, :]).astype(jnp.int32)
    csum = jnp.cumsum(onehot, axis=0)
    rank = jnp.sum((csum - onehot) * onehot, axis=1)
    counts = csum[-1]
    padded = ((counts + tm - 1) // tm) * tm
    ends = jnp.cumsum(padded)
    starts = ends - padded
    dest = starts[e_flat] + rank
    p = TOP_K * t + N_EXPERTS * tm
    tok = jnp.tile(jnp.arange(t, dtype=jnp.int32), TOP_K)
    pair = jnp.stack([tok, lax.bitcast_convert_type(wts.reshape(-1), jnp.int32)], axis=1)
    sorted_pair = jnp.zeros((p, 2), jnp.int32).at[dest].set(pair)
    sorted_tok = sorted_pair[:, 0]
    sorted_w = lax.bitcast_convert_type(sorted_pair[:, 1], F32)
    tile_start = jnp.arange(p // tm, dtype=jnp.int32) * tm
    tile_valid = (tile_start < ends[-1]).astype(jnp.int32)
    tile_expert = jnp.minimum(jnp.sum((tile_start[:, None] >= ends[None, :]).astype(jnp.int32), axis=1),
                              N_EXPERTS - 1)
    last_e = jnp.max(jnp.where(counts > 0, jnp.arange(N_EXPERTS, dtype=jnp.int32), 0))
    tile_expert = jnp.where(tile_valid != 0, tile_expert, last_e)
    xs = jnp.take(u, sorted_tok, axis=0)
    ys = _experts(tile_expert, tile_valid, xs, sorted_w[:, None], w_in, w_out, tm)
    dest2 = dest.reshape(TOP_K, t)
    return x + jnp.take(ys, dest2[0], axis=0) + jnp.take(ys, dest2[1], axis=0)


def _norm_body(x_ref, g_ref, o_ref):
    o_ref[...] = _rms(x_ref[...], g_ref[...])


def _final_norm(x, g, tm=1024):
    t = x.shape[0]
    return pl.pallas_call(
        _norm_body,
        grid=(t // tm,),
        in_specs=[pl.BlockSpec((tm, D_MODEL), lambda i: (i, 0)), pl.BlockSpec((1, D_MODEL), lambda i: (0, 0))],
        out_specs=pl.BlockSpec((tm, D_MODEL), lambda i: (i, 0)),
        out_shape=jax.ShapeDtypeStruct((t, D_MODEL), F32),
        compiler_params=_cparams(("parallel",)),
        name="final_norm",
    )(x, g)


RET_GROUP = 8


def _ret_tables(l):
    half = RET_DK // 2
    inv_freq = ROPE_BASE ** (-jnp.arange(half, dtype=F32) / half)
    ang = jnp.arange(l, dtype=F32)[:, None] * inv_freq[None, :]
    cos, sin = jnp.cos(ang), jnp.sin(ang)
    cosf = jnp.concatenate([cos, cos], axis=1)
    sinf = jnp.concatenate([-sin, sin], axis=1)
    c = RET_CHUNK
    log_gamma = jnp.log(1.0 - jnp.exp2(-5.0 - jnp.arange(RET_HEADS, dtype=F32)))[:, None, None]
    idx = jnp.arange(c, dtype=F32)
    dmat = jnp.exp(jnp.abs(idx[:, None] - idx[None, :])[None] * log_gamma)
    col = lambda e: jnp.broadcast_to(jnp.exp(e[None, :, None] * log_gamma), (RET_HEADS, c, 128))
    sc = jnp.stack([col(idx + 1.0), col(c - idx), col(c - 1.0 - idx), col(idx),
                    jnp.broadcast_to(jnp.exp(c * log_gamma), (RET_HEADS, c, 128))], axis=1)
    return cosf, sinf, dmat, sc


def _ret_body(q_ref, k_ref, v_ref, g_ref, cos_ref, sin_ref, dm_ref, sc_ref, o_ref, qs, qd, ks, vs, uf, ub):
    c = RET_CHUNK
    n_chunks = q_ref.shape[0] // c
    q_in, q_out, k_in, k_out = sc_ref[0], sc_ref[1], sc_ref[2], sc_ref[3]
    chunk_decay = sc_ref[4]

    def rot(x, cos, sin):
        return x * cos + pltpu.roll(x, RET_DK // 2, 1) * sin

    grp = RET_GROUP

    def prep(i, carry):
        ks_f, ks_b, vs_ = [], [], []
        for j in range(grp):
            rows = pl.ds(pl.multiple_of((i * grp + j) * c, c), c)
            cos, sin = cos_ref[rows, :], sin_ref[rows, :]
            q = rot(q_ref[rows, :], cos, sin)
            k = rot(k_ref[rows, :], cos, sin) * (RET_DK ** -0.5)
            v = v_ref[rows, :].astype(BF16)
            qs[rows, :] = q.astype(BF16)
            qd[rows, :] = jnp.concatenate([q * q_in, q * q_out], axis=1).astype(BF16)
            ks[rows, :] = k.astype(BF16)
            vs[rows, :] = v
            ks_f.append((k * k_in).T.astype(BF16))
            ks_b.append((k * k_out).T.astype(BF16))
            vs_.append(v)
        for j in range(grp):
            uf[i * grp + j] = jnp.dot(ks_f[j], vs_[j], preferred_element_type=F32)
            ub[i * grp + j] = jnp.dot(ks_b[j], vs_[j], preferred_element_type=F32)
        return carry

    lax.fori_loop(0, n_chunks // grp, prep, 0)

    def fwd_state(n, s):
        u = uf[n]
        uf[n] = s
        return s * chunk_decay + u

    lax.fori_loop(0, n_chunks, fwd_state, jnp.zeros((RET_DK, RET_DV), F32))

    def bwd_state(i, s):
        n = n_chunks - 1 - i
        u = ub[n]
        ub[n] = s
        return s * chunk_decay + u

    lax.fori_loop(0, n_chunks, bwd_state, jnp.zeros((RET_DK, RET_DV), F32))

    def out(i, carry):
        rows = [pl.ds(pl.multiple_of((i * grp + j) * c, c), c) for j in range(grp)]
        ss = [_dot_nt(qs[r, :], ks[r, :]) for r in rows]
        ss = [(s * dm_ref[...]).astype(BF16) for s in ss]
        states = [jnp.concatenate([uf[i * grp + j], ub[i * grp + j]], axis=0).astype(BF16) for j in range(grp)]
        os_ = [jnp.dot(s, vs[r, :], preferred_element_type=F32) for s, r in zip(ss, rows)]
        os_ = [o + jnp.dot(qd[r, :], st, preferred_element_type=F32) for o, r, st in zip(os_, rows, states)]
        for o, r in zip(os_, rows):
            mu = jnp.mean(o, axis=-1, keepdims=True)
            d = o - mu
            o = d * lax.rsqrt(jnp.mean(d * d, axis=-1, keepdims=True) + RMS_EPS)
            g = g_ref[r, :]
            o_ref[r, :] = g * _sigmoid(g) * o
        return carry

    lax.fori_loop(0, n_chunks // grp, out, 0)


def retention(proj, b, l, col0):
    cosf, sinf, dmat, sc = _ret_tables(l)
    cb = col0 // 128
    nh = RET_HEADS
    c = RET_CHUNK
    seq = lambda off: pl.BlockSpec((l, 128), lambda i, h: (i, cb + off * nh + h))
    const2 = pl.BlockSpec((l, 128), lambda i, h: (0, 0))
    return pl.pallas_call(
        _ret_body,
        grid=(b, nh),
        in_specs=[seq(0), seq(1), seq(2), seq(3), const2, const2,
                  pl.BlockSpec((None, c, c), lambda i, h: (h, 0, 0)),
                  pl.BlockSpec((None, 5, c, 128), lambda i, h: (h, 0, 0, 0))],
        out_specs=pl.BlockSpec((l, 128), lambda i, h: (i, h)),
        out_shape=jax.ShapeDtypeStruct((b * l, nh * RET_DV), F32),
        scratch_shapes=[pltpu.VMEM((l, 128), BF16), pltpu.VMEM((l, 256), BF16), pltpu.VMEM((l, 128), BF16),
                        pltpu.VMEM((l, 128), BF16), pltpu.VMEM((l // c, 128, 128), F32),
                        pltpu.VMEM((l // c, 128, 128), F32)],
        compiler_params=_cparams(("parallel", "parallel")),
        name="retention",
    )(proj, proj, proj, proj, cosf, sinf, dmat, sc)


GDN_ROWS = 512
HALO = 8


def _cumsum_rows(x, reverse):
    n = x.shape[0]
    row = lax.broadcasted_iota(jnp.int32, x.shape, 0)
    s = 1
    while s < n:
        if reverse:
            x = x + jnp.where(row < n - s, pltpu.roll(x, n - s, 0), 0.0)
        else:
            x = x + jnp.where(row >= s, pltpu.roll(x, s, 0), 0.0)
        s *= 2
    return x


def _gdn_local(insts):
    c = GDN_CHUNK
    row = lax.broadcasted_iota(jnp.int32, (c, c), 0)
    col = lax.broadcasted_iota(jnp.int32, (c, c), 1)
    pre = []
    for q, k, v, log_g, beta, d in insts:
        reverse = d == 1
        cum = _cumsum_rows(jnp.broadcast_to(log_g, (c, 128)), reverse)
        total = cum[0:1, :] if reverse else cum[c - 1:c, :]
        cum_sq = cum[:, :c]
        cum_row = jnp.sum(jnp.where(row == col, cum_sq, 0.0), axis=0, keepdims=True)
        incl = (row <= col) if reverse else (row >= col)
        strict = (row < col) if reverse else (row > col)
        decay = jnp.where(incl, jnp.exp(jnp.where(incl, cum_sq - cum_row, 0.0)), 0.0)
        pre.append((cum, total, decay, strict, k * beta, v * beta))
    qks = [_dot_nt(jnp.concatenate([q, p[4]], axis=0), k) for (q, k, *_), p in zip(insts, pre)]
    bps = [-jnp.where(p[3], qk[c:] * p[2], 0.0) for qk, p in zip(qks, pre)]
    rs = list(bps)
    for _ in range(5):
        bps = [_dot(bp, bp) for bp in bps]
        rs = [r + bp + _dot(r, bp) for r, bp in zip(rs, bps)]
    rhss = [jnp.concatenate([p[5], p[4] * jnp.exp(p[0])], axis=1) for p in pre]
    sols = [rhs + _dot(r, rhs) for r, rhs in zip(rs, rhss)]
    kdts, attns = [], []
    for (q, k, *_), p, qk in zip(insts, pre, qks):
        cum, total, decay = p[0], p[1], p[2]
        k_dec = k * jnp.exp(total - cum)
        kdts.append(jnp.concatenate([k_dec, jnp.zeros((c, 128), F32)], axis=0).T[:, :c])
        attns.append(qk[:c] * decay)
    kus = [_dot(kd_t, sol) for kd_t, sol in zip(kdts, sols)]
    aos = [_dot(attn, sol) for attn, sol in zip(attns, sols)]
    outs = []
    for (q, *_), p, ku, ao in zip(insts, pre, kus, aos):
        q_eff = q * jnp.exp(p[0]) - ao[:, GDN_DV:]
        outs.append((jnp.concatenate([ku[:, GDN_DV:], q_eff], axis=0), ku[:, :GDN_DV], ao[:, :GDN_DV], jnp.exp(p[1])))
    return outs


def _gdn_body(alog_ref, dtb_ref, q_ref, k_ref, v_ref, gate_ref, gab_ref, cwq_ref, cwk_ref, cwv_ref, onorm_ref,
              o_ref, pad, km_s, u_s, o0_s, gl_s, sf, sb):
    c = GDN_CHUNK
    l = q_ref.shape[0]
    h = pl.program_id(1)
    n_chunks = l // c
    per_block = GDN_ROWS // c

    zero_halo = jnp.zeros((HALO, 128), F32)
    for a, src in enumerate((q_ref, k_ref, v_ref)):
        pad[a, 0:HALO, :] = zero_halo
        pad[a, l + HALO:l + 2 * HALO, :] = zero_halo
        pad[a, HALO:l + HALO, :] = src[...]

    lane = lax.broadcasted_iota(jnp.int32, (1, 4), 1)
    a_row = jnp.where(lane == 0, alog_ref[0, h], alog_ref[1, h])
    dtb_row = jnp.where(lane == 0, dtb_ref[0, h], dtb_ref[1, h])
    neg_a_row = -jnp.exp(a_row)
    win_rows = GDN_ROWS + 2 * HALO

    def local_block(i, carry):
        s0 = pl.multiple_of(i * GDN_ROWS, GDN_ROWS)
        ys = []
        for a, cw_ref in enumerate((cwq_ref, cwk_ref, cwv_ref)):
            win = pad[a, pl.ds(s0, win_rows), :]
            acc = cw_ref[2:3, :] * win
            for j in (0, 1, 3, 4):
                acc = acc + cw_ref[j:j + 1, :] * pltpu.roll(win, (GDN_CONV // 2 - j) % win_rows, 0)
            y = acc[HALO:HALO + GDN_ROWS]
            y = y * _sigmoid(y)
            if a < 2:
                y = y * lax.rsqrt(jnp.sum(y * y, axis=-1, keepdims=True) + 1e-6)
            if a == 0:
                y = y * (GDN_DK ** -0.5)
            ys.append(y)
        ab = gab_ref[pl.ds(s0, GDN_ROWS), :]
        x = ab + dtb_row
        log_g = neg_a_row * (jnp.maximum(x, 0.0) + jnp.log(1.0 + jnp.exp(-jnp.abs(x))))
        beta = _sigmoid(ab)
        insts = []
        for j in range(per_block):
            sl = slice(j * c, (j + 1) * c)
            for d in range(2):
                insts.append((ys[0][sl], ys[1][sl], ys[2][sl], log_g[sl, d:d + 1], beta[sl, 2 + d:3 + d], d))
        outs = _gdn_local(insts)
        for idx, (km, u, o0, g_last) in enumerate(outs):
            m = (idx % 2) * n_chunks + i * per_block + idx // 2
            km_s[m] = km.astype(BF16)
            u_s[m] = u.astype(BF16)
            o0_s[m] = o0
            gl_s[m] = jnp.broadcast_to(g_last, (8, 128))
        return carry

    lax.fori_loop(0, l // GDN_ROWS, local_block, 0)

    sf[...] = jnp.zeros_like(sf)
    sb[...] = jnp.zeros_like(sb)

    def scan(i, carry):
        for d, s_ref in ((0, sf), (1, sb)):
            n = i if d == 0 else n_chunks - 1 - i
            m = d * n_chunks + n
            rows = pl.ds(pl.multiple_of(n * c, c) + HALO, c)
            s = s_ref[...]
            r = jnp.dot(km_s[m], s.astype(BF16), preferred_element_type=F32)
            pad[d, rows, :] = o0_s[m] + r[GDN_DK:]
            s_ref[...] = s * gl_s[m][0:1, :] - r[:GDN_DK] + u_s[m].astype(F32)
        return carry

    lax.fori_loop(0, n_chunks, scan, 0)

    def fin(i, carry):
        r0 = pl.multiple_of(i * GDN_ROWS, GDN_ROWS)
        o = pad[0, pl.ds(r0 + HALO, GDN_ROWS), :] + pad[1, pl.ds(r0 + HALO, GDN_ROWS), :]
        o = o * lax.rsqrt(jnp.mean(o * o, axis=-1, keepdims=True) + RMS_EPS) * onorm_ref[...]
        g = gate_ref[pl.ds(r0, GDN_ROWS), :]
        o_ref[pl.ds(r0, GDN_ROWS), :] = o * (g * _sigmoid(g))
        return carry

    lax.fori_loop(0, l // GDN_ROWS, fin, 0)


def gdn(proj, gabh, conv_w, a_log, dt_bias, onorm, b, l, col0):
    cb = col0 // 128
    nh = GDN_HEADS
    nc = l // GDN_CHUNK
    seq = lambda off: pl.BlockSpec((l, 128), lambda i, h, *_: (i, cb + off * nh + h))
    cw = lambda off: pl.BlockSpec((GDN_CONV, 128), lambda i, h, *_: (0, off * nh + h))
    grid_spec = pltpu.PrefetchScalarGridSpec(
        num_scalar_prefetch=2,
        grid=(b, nh),
        in_specs=[seq(0), seq(1), seq(2), seq(3),
                  pl.BlockSpec((None, l, 4), lambda i, h, *_: (h, i, 0)),
                  cw(0), cw(1), cw(2),
                  pl.BlockSpec((1, 128), lambda i, h, *_: (0, 0))],
        out_specs=pl.BlockSpec((l, 128), lambda i, h, *_: (i, h)),
        scratch_shapes=[pltpu.VMEM((3, l + 2 * HALO, 128), F32),
                        pltpu.VMEM((2 * nc, GDN_DK + GDN_CHUNK, GDN_DK), BF16),
                        pltpu.VMEM((2 * nc, GDN_DK, GDN_DV), BF16),
                        pltpu.VMEM((2 * nc, GDN_CHUNK, GDN_DV), F32),
                        pltpu.VMEM((2 * nc, 8, 128), F32),
                        pltpu.VMEM((GDN_DK, GDN_DV), F32), pltpu.VMEM((GDN_DK, GDN_DV), F32)],
    )
    return pl.pallas_call(
        _gdn_body,
        grid_spec=grid_spec,
        out_shape=jax.ShapeDtypeStruct((b * l, nh * GDN_DV), F32),
        compiler_params=_cparams(("parallel", "parallel")),
        name="gated_deltanet",
    )(a_log, dt_bias, proj, proj, proj, proj, gabh, conv_w, conv_w, conv_w, onorm[None, :])


HY_N1 = 64
HY_ROWS = 256


def _hyconv_body(x0_ref, x1_ref, v_ref, c0_ref, c1_ref, cv_ref, z_ref, x0c_ref, pad):
    l = x0_ref.shape[0]
    zero_halo = jnp.zeros((HALO, 128), F32)
    for a, src in enumerate((x0_ref, x1_ref, v_ref)):
        pad[a, 0:HALO, :] = zero_halo
        pad[a, l + HALO:l + 2 * HALO, :] = zero_halo
        pad[a, HALO:l + HALO, :] = src[...]
    win_rows = HY_ROWS + 2 * HALO

    def block(i, carry):
        s0 = pl.multiple_of(i * HY_ROWS, HY_ROWS)
        outs = []
        for a, cw_ref in enumerate((c0_ref, c1_ref, cv_ref)):
            win = pad[a, pl.ds(s0, win_rows), :]
            acc = cw_ref[1:2, :] * win
            acc = acc + cw_ref[0:1, :] * pltpu.roll(win, 1, 0)
            acc = acc + cw_ref[2:3, :] * pltpu.roll(win, win_rows - 1, 0)
            outs.append(acc[HALO:HALO + HY_ROWS])
        rows = pl.ds(s0, HY_ROWS)
        x0c_ref[rows, :] = outs[0]
        z_ref[rows, :] = outs[2] * outs[1]
        return carry

    lax.fori_loop(0, l // HY_ROWS, block, 0)


def _hy_conv(proj, conv_w, b, l, col0):
    cb = col0 // 128
    nc = HY_WIDTH // 128
    seq = lambda off: pl.BlockSpec((l, 128), lambda i, c: (i, cb + off * nc + c))
    cw = lambda off: pl.BlockSpec((HY_SHORT, 128), lambda i, c: (0, off * nc + c))
    out = pl.BlockSpec((l, 128), lambda i, c: (i, c))
    return pl.pallas_call(
        _hyconv_body,
        grid=(b, nc),
        in_specs=[seq(0), seq(1), seq(2), cw(0), cw(1), cw(2)],
        out_specs=[out, out],
        out_shape=[jax.ShapeDtypeStruct((b * l, HY_WIDTH), F32), jax.ShapeDtypeStruct((b * l, HY_WIDTH), F32)],
        scratch_shapes=[pltpu.VMEM((3, l + 2 * HALO, 128), F32)],
        compiler_params=_cparams(("parallel", "parallel")),
        name="hyena_short_conv",
    )(proj, proj, proj, conv_w, conv_w, conv_w)


def _hyfilt_body(feat_ref, w1_ref, b1_ref, w2_ref, b2_ref, w3_ref, b3_ref, fr_ref, wo_ref, dec_ref, o_ref, *, l):
    tm = feat_ref.shape[0]
    hdot = lambda a, b_: jnp.dot(a, b_, precision=HI, preferred_element_type=F32)
    feat = feat_ref[...]
    fr = fr_ref[...]
    h = jnp.sin(fr * (hdot(feat, w1_ref[...]) + b1_ref[...]))
    h = jnp.sin(fr * (hdot(h, w2_ref[...]) + b2_ref[...]))
    h = jnp.sin(fr * (hdot(h, w3_ref[...]) + b3_ref[...]))
    h = hdot(h, wo_ref[...]) * jnp.exp(-feat[:, 0:1] * jnp.abs(dec_ref[...]))
    m = pl.program_id(0) * tm + lax.broadcasted_iota(jnp.int32, (tm, 1), 0)
    o_ref[...] = jnp.where(m == l, 0.0, h)


def _hy_filter(l, w1, b1, w2, b2, w3, b3, freq, wout, decay, tm=512):
    n = 2 * l
    m = jnp.arange(n, dtype=jnp.int32)
    t = jnp.where(m < l, m, n - m).astype(F32)[:, None]
    t01 = t / (l - 1)
    bands = (HY_EMB - 1) // 2
    f = jnp.linspace(1e-4, bands - 1, bands, dtype=F32)[None, :]
    w = 2.0 * math.pi * t / l
    feat = jnp.concatenate([t01, jnp.cos(f * w), -jnp.sin(f * w)], axis=-1)
    const = lambda i: (0, 0)
    half = lambda i: (0, (i * tm) // l)
    vec = lambda x: x[None, :]
    return pl.pallas_call(
        functools.partial(_hyfilt_body, l=l),
        grid=(n // tm,),
        in_specs=[pl.BlockSpec((tm, HY_EMB), lambda i: (i, 0)),
                  pl.BlockSpec((HY_EMB, HY_FFN), const), pl.BlockSpec((1, HY_FFN), const),
                  pl.BlockSpec((HY_FFN, HY_FFN), const), pl.BlockSpec((1, HY_FFN), const),
                  pl.BlockSpec((HY_FFN, HY_FFN), const), pl.BlockSpec((1, HY_FFN), const),
                  pl.BlockSpec((1, HY_FFN), const),
                  pl.BlockSpec((HY_FFN, HY_WIDTH), half),
                  pl.BlockSpec((None, 1, HY_WIDTH), lambda i: ((i * tm) // l, 0, 0))],
        out_specs=pl.BlockSpec((tm, HY_WIDTH), lambda i: (i, 0)),
        out_shape=jax.ShapeDtypeStruct((n, HY_WIDTH), F32),
        compiler_params=_cparams(("parallel",)),
        name="hyena_filter",
    )(feat, w1, vec(b1), w2, vec(b2), w3, vec(b3), vec(freq), wout, decay[:, None, :])


def _dft_tables(l):
    n = 2 * l
    n1, n2 = HY_N1, n // HY_N1
    ang = lambda num, den: (2.0 * math.pi / den) * (num % den).astype(F32)
    i1 = jnp.arange(n1, dtype=jnp.int32)
    a1 = ang(i1[:, None] * i1[None, :], n1)
    fwd1 = jnp.concatenate([jnp.cos(a1), -jnp.sin(a1)], axis=0)
    a1h = a1[:, :n1 // 2].T
    inv1 = jnp.concatenate([jnp.cos(a1h), -jnp.sin(a1h)], axis=1) / n
    i2 = jnp.arange(n2, dtype=jnp.int32)
    freq = i1[:, None, None] + n1 * i2[None, :, None]
    a2 = ang(freq * i2[None, None, :], n)
    gr, gi = jnp.cos(a2), -jnp.sin(a2)
    fwd2 = jnp.concatenate([jnp.concatenate([gr, -gi], axis=2), jnp.concatenate([gi, gr], axis=2)], axis=1)
    inv2 = jnp.swapaxes(fwd2, 1, 2)
    return fwd1, inv1, fwd2, inv2


HY_TR = 8


def _dft1_body(x_ref, f_ref, o_ref, *, exact):
    k1, tr, c = x_ref.shape
    x = x_ref[...].reshape(k1 * tr, c)
    if exact:
        o = jnp.dot(f_ref[...], x, precision=HI, preferred_element_type=F32)
    else:
        o = jnp.dot(f_ref[...], x.astype(BF16), preferred_element_type=F32)
    o_ref[...] = o.astype(o_ref.dtype).reshape(o_ref.shape)


def _dft1(x4, mat, out_dtype, exact):
    bt, k1, n2, c = x4.shape
    n1 = mat.shape[0] // 2
    kron = jnp.kron(mat, jnp.eye(HY_TR, dtype=mat.dtype))
    return pl.pallas_call(
        functools.partial(_dft1_body, exact=exact),
        grid=(bt, n2 // HY_TR),
        in_specs=[pl.BlockSpec((None, k1, HY_TR, c), lambda i, j: (i, 0, j, 0)),
                  pl.BlockSpec((2 * n1 * HY_TR, k1 * HY_TR), lambda i, j: (0, 0))],
        out_specs=pl.BlockSpec((None, 2, n1, HY_TR, c), lambda i, j: (i, 0, 0, j, 0)),
        out_shape=jax.ShapeDtypeStruct((bt, 2, n1, n2, c), out_dtype),
        compiler_params=_cparams(("parallel", "parallel")),
        name="hyena_dft_stage1",
    )(x4, kron)


def _spec2_body(a_ref, g_ref, o_ref):
    n2 = a_ref.shape[1]
    a = a_ref[...].reshape(2 * n2, a_ref.shape[2])
    o = jnp.dot(g_ref[...], a, precision=HI, preferred_element_type=F32)
    o_ref[...] = o.reshape(o_ref.shape)


def _filter_spectrum(circ, fwd1, fwd2, l):
    n1, n2 = HY_N1, 2 * l // HY_N1
    a5 = _dft1(circ.reshape(1, n1, n2, HY_WIDTH), fwd1, F32, True)[0]
    return pl.pallas_call(
        _spec2_body,
        grid=(n1,),
        in_specs=[pl.BlockSpec((2, None, n2, HY_WIDTH), lambda i: (0, i, 0, 0)),
                  pl.BlockSpec((None, 2 * n2, 2 * n2), lambda i: (i, 0, 0))],
        out_specs=pl.BlockSpec((None, 2, n2, HY_WIDTH), lambda i: (i, 0, 0, 0)),
        out_shape=jax.ShapeDtypeStruct((n1, 2, n2, HY_WIDTH), F32),
        compiler_params=_cparams(("parallel",)),
        name="hyena_filter_spectrum",
    )(a5, fwd2)


def _hymid_body(a_ref, g_ref, gi_ref, h_ref, o_ref):
    bb, _, n2, c = a_ref.shape
    a = jnp.concatenate([a_ref[i].reshape(2 * n2, c) for i in range(bb)], axis=1)
    x = jnp.dot(g_ref[...], a, preferred_element_type=F32)
    hr = jnp.concatenate([h_ref[0]] * bb, axis=1)
    hi = jnp.concatenate([h_ref[1]] * bb, axis=1)
    xr, xi = x[:n2], x[n2:]
    y = jnp.concatenate([xr * hr - xi * hi, xr * hi + xi * hr], axis=0).astype(BF16)
    o = jnp.dot(gi_ref[...], y, preferred_element_type=F32).astype(o_ref.dtype)
    for i in range(bb):
        o_ref[i] = o[:, i * c:(i + 1) * c].reshape(2, n2, c)


def _hy_mid(a5, fwd2, inv2, spec):
    b, _, n1, n2, c = a5.shape
    bb = 8 if b % 8 == 0 else 4
    blk = pl.BlockSpec((bb, 2, None, n2, c), lambda f, i: (i, 0, f, 0, 0))
    mat = pl.BlockSpec((None, 2 * n2, 2 * n2), lambda f, i: (f, 0, 0))
    return pl.pallas_call(
        _hymid_body,
        grid=(n1, b // bb),
        in_specs=[blk, mat, mat, pl.BlockSpec((None, 2, n2, c), lambda f, i: (f, 0, 0, 0))],
        out_specs=blk,
        out_shape=jax.ShapeDtypeStruct(a5.shape, BF16),
        compiler_params=_cparams(("parallel", "parallel")),
        name="hyena_spectral_product",
    )(a5, fwd2, inv2, spec)


def _hyfin_body(b_ref, m_ref, z_ref, x0_ref, bias_ref, o_ref):
    k1, tr, c = z_ref.shape
    y = jnp.dot(m_ref[...], b_ref[...].reshape(m_ref.shape[1], c), preferred_element_type=F32).reshape(k1, tr, c)
    o_ref[...] = (y + z_ref[...] * bias_ref[...]) * x0_ref[...]


def _hy_final(bq5, inv1, z4, x04, bias):
    b, _, n1, n2, c = bq5.shape
    k1 = z4.shape[1]
    kron = jnp.kron(inv1, jnp.eye(HY_TR, dtype=inv1.dtype))
    row = pl.BlockSpec((None, k1, HY_TR, c), lambda i, j: (i, 0, j, 0))
    return pl.pallas_call(
        _hyfin_body,
        grid=(b, n2 // HY_TR),
        in_specs=[pl.BlockSpec((None, 2, n1, HY_TR, c), lambda i, j: (i, 0, 0, j, 0)),
                  pl.BlockSpec((k1 * HY_TR, 2 * n1 * HY_TR), lambda i, j: (0, 0)), row, row,
                  pl.BlockSpec((1, 1, c), lambda i, j: (0, 0, 0))],
        out_specs=row,
        out_shape=jax.ShapeDtypeStruct(z4.shape, F32),
        compiler_params=_cparams(("parallel", "parallel")),
        name="hyena_dft_final",
    )(bq5, kron, z4, x04, bias[None, None, :])


def hyena(proj, conv_w, w1, b1, w2, b2, w3, b3, freq, wout, decay, bias, b, l, col0):
    n1, n2, c = HY_N1, 2 * l // HY_N1, HY_WIDTH
    fwd1, inv1, fwd2, inv2 = _dft_tables(l)
    circ = _hy_filter(l, w1, b1, w2, b2, w3, b3, freq, wout, decay)
    spec = _filter_spectrum(circ, fwd1, fwd2, l)
    z, x0c = _hy_conv(proj, conv_w, b, l, col0)
    z4 = z.reshape(b, n1 // 2, n2, c)
    a5 = _dft1(z4, fwd1[:, :n1 // 2].astype(BF16), BF16, False)
    bq5 = _hy_mid(a5, fwd2.astype(BF16), inv2.astype(BF16), spec)
    out = _hy_final(bq5, inv1.astype(BF16), z4, x0c.reshape(b, n1 // 2, n2, c), bias)
    return out.reshape(b * l, c)


def _reorder_w_in(w):
    rest, gab, hy, gates = w[:, :4096], w[:, 4096:4112], w[:, 4112:5648], w[:, 5648:]
    pad = jnp.zeros((D_MODEL, D_PROJ - COL_GAB - 16), w.dtype)
    return jnp.concatenate([gates, rest, hy, gab, pad], axis=1).astype(BF16)


def _trunk(x3, p):
    b, l, _ = x3.shape
    t = b * l
    x = x3.reshape(t, D_MODEL)
    for i in range(DEPTH):
        proj = _proj(x, p['norm_mix'][i][None, :], p['w_in_r'][i])
        y_ret = retention(proj, b, l, COL_RET)
        gabh = jnp.transpose(proj[:, COL_GAB:COL_GAB + 4 * GDN_HEADS].reshape(t, 4, GDN_HEADS), (2, 0, 1))
        y_gdn = gdn(proj, gabh, p['gdn_conv_w'][i], p['gdn_A_log'][i], p['gdn_dt_bias'][i], p['gdn_onorm'][i],
                    b, l, COL_GDN)
        y_hy = hyena(proj, p['hy_conv_w'][i], p['hy_w1'][i], p['hy_b1'][i], p['hy_w2'][i], p['hy_b2'][i],
                     p['hy_w3'][i], p['hy_b3'][i], p['hy_freq'][i], p['hy_wout'][i], p['hy_decay'][i],
                     p['hy_bias'][i], b, l, COL_HY)
        x = _merge(x, proj, y_ret, y_gdn, y_hy, p['w_ret_o_b'][i], p['w_gdn_o_b'][i], p['w_hy_o_b'][i], p['w_o_b'][i])
        g = p['norm_ffn'][i][None, :]
        if i % 2 == 0:
            x = _ffn(x, g, p['dense_w_in_b'][i // 2], p['dense_w_out_b'][i // 2])
        else:
            x = _moe(x, g, p['moe_router_t'][i // 2], p['moe_w_in_b'][i // 2], p['moe_w_out_b'][i // 2])
    return _final_norm(x, p['norm_final'][None, :]).reshape(b, l, D_MODEL)


def kernel(x_prompt, x_sample, norm_mix, w_in, gdn_conv_w, gdn_A_log, gdn_dt_bias, gdn_onorm, hy_conv_w, hy_w1, hy_b1, hy_w2, hy_b2, hy_w3, hy_b3, hy_freq, hy_wout, hy_decay, hy_bias, w_ret_o, w_gdn_o, w_hy_o, w_o, norm_ffn, dense_w_in, dense_w_out, moe_router, moe_w_in, moe_w_out, norm_final):
    p = dict(norm_mix=norm_mix, gdn_conv_w=gdn_conv_w, gdn_A_log=gdn_A_log, gdn_dt_bias=gdn_dt_bias,
             gdn_onorm=gdn_onorm, hy_conv_w=hy_conv_w, hy_w1=hy_w1, hy_b1=hy_b1, hy_w2=hy_w2, hy_b2=hy_b2,
             hy_w3=hy_w3, hy_b3=hy_b3, hy_freq=hy_freq, hy_wout=hy_wout, hy_decay=hy_decay, hy_bias=hy_bias,
             norm_ffn=norm_ffn, norm_final=norm_final)
    p['w_in_r'] = jax.vmap(_reorder_w_in)(w_in)
    p['w_ret_o_b'] = w_ret_o.astype(BF16)
    p['w_gdn_o_b'] = w_gdn_o.astype(BF16)
    p['w_hy_o_b'] = w_hy_o.astype(BF16)
    p['w_o_b'] = w_o.astype(BF16)
    p['dense_w_in_b'] = dense_w_in.astype(BF16)
    p['dense_w_out_b'] = dense_w_out.astype(BF16)
    p['moe_router_t'] = jnp.swapaxes(moe_router, 1, 2)
    p['moe_w_in_b'] = moe_w_in.astype(BF16)
    p['moe_w_out_b'] = moe_w_out.astype(BF16)
    return (_trunk(x_prompt, p), _trunk(x_sample, p))
```

```python
import functools
import math

import jax
import jax.numpy as jnp
from jax import lax
from jax.experimental import pallas as pl
from jax.experimental.pallas import tpu as pltpu

F32 = jnp.float32
BF16 = jnp.bfloat16

D_MODEL = 1024
DEPTH = 2
RMS_EPS = 1e-6
RET_HEADS, RET_DK, RET_DV, RET_CHUNK = 4, 128, 128, 128
ROPE_BASE = 10000.0
GDN_HEADS, GDN_DK, GDN_DV, GDN_CHUNK, GDN_CONV = 4, 128, 128, 64, 5
HY_WIDTH, HY_SHORT, HY_EMB, HY_FFN = 512, 3, 33, 64
FF_DENSE = 2816
N_EXPERTS, TOP_K, FF_EXPERT = 8, 2, 3584

COL_GATES = 0
COL_RET = 3072
COL_GDN = 5120
COL_HY = 7168
COL_GAB = 8704
D_PROJ = 8832

VMEM_LIMIT = 56 * 1024 * 1024
HI = lax.Precision.HIGHEST


def _cparams(sem):
    return pltpu.CompilerParams(dimension_semantics=sem, vmem_limit_bytes=VMEM_LIMIT)


def _rms(x, g):
    return x * lax.rsqrt(jnp.mean(x * x, axis=-1, keepdims=True) + RMS_EPS) * g


def _sigmoid(x):
    return 1.0 / (1.0 + jnp.exp(-x))


def _bdot(a, b):
    return jnp.dot(a.astype(BF16), b, preferred_element_type=F32)


def _dot(a, b):
    return jnp.dot(a.astype(BF16), b.astype(BF16), preferred_element_type=F32)


def _dot_nt(a, b):
    return lax.dot_general(a.astype(BF16), b.astype(BF16), (((1,), (1,)), ((), ())), preferred_element_type=F32)


def _proj_body(x_ref, g_ref, w_ref, o_ref, u_ref):
    @pl.when(pl.program_id(1) == 0)
    def _():
        u_ref[...] = _rms(x_ref[...], g_ref[...]).astype(BF16)

    o_ref[...] = jnp.dot(u_ref[...], w_ref[...], preferred_element_type=F32)


def _proj(x, g, w, tm=1024, tn=2944):
    t = x.shape[0]
    return pl.pallas_call(
        _proj_body,
        grid=(t // tm, D_PROJ // tn),
        in_specs=[pl.BlockSpec((tm, D_MODEL), lambda i, j: (i, 0)),
                  pl.BlockSpec((1, D_MODEL), lambda i, j: (0, 0)),
                  pl.BlockSpec((D_MODEL, tn), lambda i, j: (0, j))],
        out_specs=pl.BlockSpec((tm, tn), lambda i, j: (i, j)),
        out_shape=jax.ShapeDtypeStruct((t, D_PROJ), F32),
        scratch_shapes=[pltpu.VMEM((tm, D_MODEL), BF16)],
        compiler_params=_cparams(("parallel", "arbitrary")),
        name="in_proj",
    )(x, g, w)


def _merge_body(x_ref, yr_ref, yg_ref, yh_ref, gr_ref, gg_ref, gh_ref, wr_ref, wg_ref, wh_ref, wo_ref, o_ref):
    merged = (_sigmoid(gr_ref[...]) * _bdot(yr_ref[...], wr_ref[...])
              + _sigmoid(gg_ref[...]) * _bdot(yg_ref[...], wg_ref[...])
              + _sigmoid(gh_ref[...]) * _bdot(yh_ref[...], wh_ref[...]))
    o_ref[...] = x_ref[...] + _bdot(merged, wo_ref[...])


def _merge(x, proj, y_ret, y_gdn, y_hy, w_ret, w_gdn, w_hy, w_o, tm=512):
    t = x.shape[0]
    row = lambda i: (i, 0)
    const = lambda i: (0, 0)
    gate = lambda b: pl.BlockSpec((tm, D_MODEL), lambda i: (i, COL_GATES // D_MODEL + b))
    return pl.pallas_call(
        _merge_body,
        grid=(t // tm,),
        in_specs=[pl.BlockSpec((tm, D_MODEL), row),
                  pl.BlockSpec((tm, 512), row), pl.BlockSpec((tm, 512), row), pl.BlockSpec((tm, 512), row),
                  gate(0), gate(1), gate(2),
                  pl.BlockSpec((512, D_MODEL), const), pl.BlockSpec((512, D_MODEL), const),
                  pl.BlockSpec((512, D_MODEL), const), pl.BlockSpec((D_MODEL, D_MODEL), const)],
        out_specs=pl.BlockSpec((tm, D_MODEL), row),
        out_shape=jax.ShapeDtypeStruct((t, D_MODEL), F32),
        compiler_params=_cparams(("parallel",)),
        name="merge",
    )(x, y_ret, y_gdn, y_hy, proj, proj, proj, w_ret, w_gdn, w_hy, w_o)


def _ffn_body(x_ref, g_ref, wa_ref, wb_ref, wo_ref, o_ref, u_ref, acc_ref):
    j = pl.program_id(1)

    @pl.when(j == 0)
    def _():
        u_ref[...] = _rms(x_ref[...], g_ref[...]).astype(BF16)
        acc_ref[...] = jnp.zeros_like(acc_ref)

    u = u_ref[...]
    a = jnp.dot(u, wa_ref[...], preferred_element_type=F32)
    b = jnp.dot(u, wb_ref[...], preferred_element_type=F32)
    acc_ref[...] += _bdot(a * _sigmoid(a) * b, wo_ref[...])

    @pl.when(j == pl.num_programs(1) - 1)
    def _():
        o_ref[...] = x_ref[...] + acc_ref[...]


def _ffn(x, g, w_in, w_out, tm=512, tf=1408):
    t = x.shape[0]
    nf = FF_DENSE // tf
    return pl.pallas_call(
        _ffn_body,
        grid=(t // tm, nf),
        in_specs=[pl.BlockSpec((tm, D_MODEL), lambda i, j: (i, 0)),
                  pl.BlockSpec((1, D_MODEL), lambda i, j: (0, 0)),
                  pl.BlockSpec((D_MODEL, tf), lambda i, j: (0, j)),
                  pl.BlockSpec((D_MODEL, tf), lambda i, j: (0, j + nf)),
                  pl.BlockSpec((tf, D_MODEL), lambda i, j: (j, 0))],
        out_specs=pl.BlockSpec((tm, D_MODEL), lambda i, j: (i, 0)),
        out_shape=jax.ShapeDtypeStruct((t, D_MODEL), F32),
        scratch_shapes=[pltpu.VMEM((tm, D_MODEL), BF16), pltpu.VMEM((tm, D_MODEL), F32)],
        compiler_params=_cparams(("parallel", "arbitrary")),
        name="dense_ffn",
    )(x, g, w_in, w_in, w_out)


def _route_body(x_ref, g_ref, wr_ref, u_ref, idx_ref, wt_ref):
    u = _rms(x_ref[...], g_ref[...])
    u_ref[...] = u
    logits = lax.dot_general(wr_ref[...], u, (((1,), (1,)), ((), ())), precision=HI, preferred_element_type=F32)
    row = lax.broadcasted_iota(jnp.int32, logits.shape, 0)
    m1 = jnp.max(logits, axis=0, keepdims=True)
    i1 = jnp.min(jnp.where(logits == m1, row, N_EXPERTS), axis=0, keepdims=True)
    rest = jnp.where(row == i1, -jnp.inf, logits)
    m2 = jnp.max(rest, axis=0, keepdims=True)
    i2 = jnp.min(jnp.where(rest == m2, row, N_EXPERTS), axis=0, keepdims=True)
    e = jnp.exp(m2 - m1)
    idx_ref[0:1, :] = i1
    idx_ref[1:2, :] = i2
    wt_ref[0:1, :] = 1.0 / (1.0 + e)
    wt_ref[1:2, :] = e / (1.0 + e)


def _route(x, g, wr_t, tm=512):
    t = x.shape[0]
    return pl.pallas_call(
        _route_body,
        grid=(t // tm,),
        in_specs=[pl.BlockSpec((tm, D_MODEL), lambda i: (i, 0)),
                  pl.BlockSpec((1, D_MODEL), lambda i: (0, 0)),
                  pl.BlockSpec((N_EXPERTS, D_MODEL), lambda i: (0, 0))],
        out_specs=[pl.BlockSpec((tm, D_MODEL), lambda i: (i, 0)),
                   pl.BlockSpec((TOP_K, tm), lambda i: (0, i)),
                   pl.BlockSpec((TOP_K, tm), lambda i: (0, i))],
        out_shape=[jax.ShapeDtypeStruct((t, D_MODEL), F32),
                   jax.ShapeDtypeStruct((TOP_K, t), jnp.int32),
                   jax.ShapeDtypeStruct((TOP_K, t), F32)],
        compiler_params=_cparams(("parallel",)),
        name="moe_router",
    )(x, g, wr_t)


def _experts_body(te_ref, tv_ref, x_ref, sw_ref, wa_ref, wb_ref, wo_ref, o_ref, acc_ref, xb_ref):
    i, j = pl.program_id(0), pl.program_id(1)

    @pl.when(j == 0)
    def _():
        acc_ref[...] = jnp.zeros_like(acc_ref)
        xb_ref[...] = x_ref[...].astype(BF16)

    @pl.when(tv_ref[i] != 0)
    def _():
        x = xb_ref[...]
        a = jnp.dot(x, wa_ref[...].astype(BF16), preferred_element_type=F32)
        b = jnp.dot(x, wb_ref[...].astype(BF16), preferred_element_type=F32)
        acc_ref[...] += _bdot(a * _sigmoid(a) * b, wo_ref[...].astype(BF16))

    @pl.when(j == pl.num_programs(1) - 1)
    def _():
        o_ref[...] = acc_ref[...] * sw_ref[...]


def _experts(tile_expert, tile_valid, xs, sw, w_in, w_out, tm, tf=512):
    p = xs.shape[0]
    nf = FF_EXPERT // tf
    grid_spec = pltpu.PrefetchScalarGridSpec(
        num_scalar_prefetch=2,
        grid=(p // tm, nf),
        in_specs=[pl.BlockSpec((tm, D_MODEL), lambda i, j, te, tv: (i, 0)),
                  pl.BlockSpec((tm, 1), lambda i, j, te, tv: (i, 0)),
                  pl.BlockSpec((None, D_MODEL, tf), lambda i, j, te, tv: (te[i], 0, j)),
                  pl.BlockSpec((None, D_MODEL, tf), lambda i, j, te, tv: (te[i], 0, j + nf)),
                  pl.BlockSpec((None, tf, D_MODEL), lambda i, j, te, tv: (te[i], j, 0))],
        out_specs=pl.BlockSpec((tm, D_MODEL), lambda i, j, te, tv: (i, 0)),
        scratch_shapes=[pltpu.VMEM((tm, D_MODEL), F32), pltpu.VMEM((tm, D_MODEL), BF16)],
    )
    return pl.pallas_call(
        _experts_body,
        grid_spec=grid_spec,
        out_shape=jax.ShapeDtypeStruct((p, D_MODEL), F32),
        compiler_params=_cparams(("parallel", "arbitrary")),
        name="moe_experts",
    )(tile_expert, tile_valid, xs, sw, w_in, w_in, w_out)


def _moe(x, g, wr_t, w_in, w_out, tm=1024):
    t = x.shape[0]
    u, idx, wts = _route(x, g, wr_t)
    e_flat = idx.reshape(-1)
    onehot = (e_flat[:, None] == jnp.arange(N_EXPERTS, dtype=jnp.int32)[None, :]).astype(jnp.int32)
    csum = jnp.cumsum(onehot, axis=0)
    rank = jnp.sum((csum - onehot) * onehot, axis=1)
    counts = csum[-1]
    padded = ((counts + tm - 1) // tm) * tm
    ends = jnp.cumsum(padded)
    starts = ends - padded
    dest = starts[e_flat] + rank
    p = TOP_K * t + N_EXPERTS * tm
    tok = jnp.tile(jnp.arange(t, dtype=jnp.int32), TOP_K)
    pair = jnp.stack([tok, lax.bitcast_convert_type(wts.reshape(-1), jnp.int32)], axis=1)
    sorted_pair = jnp.zeros((p, 2), jnp.int32).at[dest].set(pair)
    sorted_tok = sorted_pair[:, 0]
    sorted_w = lax.bitcast_convert_type(sorted_pair[:, 1], F32)
    tile_start = jnp.arange(p // tm, dtype=jnp.int32) * tm
    tile_valid = (tile_start < ends[-1]).astype(jnp.int32)
    tile_expert = jnp.minimum(jnp.sum((tile_start[:, None] >= ends[None, :]).astype(jnp.int32), axis=1),
                              N_EXPERTS - 1)
    last_e = jnp.max(jnp.where(counts > 0, jnp.arange(N_EXPERTS, dtype=jnp.int32), 0))
    tile_expert = jnp.where(tile_valid != 0, tile_expert, last_e)
    xs = jnp.take(u, sorted_tok, axis=0)
    ys = _experts(tile_expert, tile_valid, xs, sorted_w[:, None], w_in, w_out, tm)
    dest2 = dest.reshape(TOP_K, t)
    return x + jnp.take(ys, dest2[0], axis=0) + jnp.take(ys, dest2[1], axis=0)


def _norm_body(x_ref, g_ref, o_ref):
    o_ref[...] = _rms(x_ref[...], g_ref[...])


def _final_norm(x, g, tm=1024):
    t = x.shape[0]
    return pl.pallas_call(
        _norm_body,
        grid=(t // tm,),
        in_specs=[pl.BlockSpec((tm, D_MODEL), lambda i: (i, 0)), pl.BlockSpec((1, D_MODEL), lambda i: (0, 0))],
        out_specs=pl.BlockSpec((tm, D_MODEL), lambda i: (i, 0)),
        out_shape=jax.ShapeDtypeStruct((t, D_MODEL), F32),
        compiler_params=_cparams(("parallel",)),
        name="final_norm",
    )(x, g)


RET_GROUP = 8


def _ret_tables(l):
    half = RET_DK // 2
    inv_freq = ROPE_BASE ** (-jnp.arange(half, dtype=F32) / half)
    ang = jnp.arange(l, dtype=F32)[:, None] * inv_freq[None, :]
    cos, sin = jnp.cos(ang), jnp.sin(ang)
    cosf = jnp.concatenate([cos, cos], axis=1)
    sinf = jnp.concatenate([-sin, sin], axis=1)
    c = RET_CHUNK
    log_gamma = jnp.log(1.0 - jnp.exp2(-5.0 - jnp.arange(RET_HEADS, dtype=F32)))[:, None, None]
    idx = jnp.arange(c, dtype=F32)
    dmat = jnp.exp(jnp.abs(idx[:, None] - idx[None, :])[None] * log_gamma)
    col = lambda e: jnp.broadcast_to(jnp.exp(e[None, :, None] * log_gamma), (RET_HEADS, c, 128))
    sc = jnp.stack([col(idx + 1.0), col(c - idx), col(c - 1.0 - idx), col(idx),
                    jnp.broadcast_to(jnp.exp(c * log_gamma), (RET_HEADS, c, 128))], axis=1)
    return cosf, sinf, dmat, sc


def _ret_body(q_ref, k_ref, v_ref, g_ref, cos_ref, sin_ref, dm_ref, sc_ref, o_ref, qs, qd, ks, vs, uf, ub):
    c = RET_CHUNK
    n_chunks = q_ref.shape[0] // c
    q_in, q_out, k_in, k_out = sc_ref[0], sc_ref[1], sc_ref[2], sc_ref[3]
    chunk_decay = sc_ref[4]

    def rot(x, cos, sin):
        return x * cos + pltpu.roll(x, RET_DK // 2, 1) * sin

    grp = RET_GROUP

    def prep(i, carry):
        ks_f, ks_b, vs_ = [], [], []
        for j in range(grp):
            rows = pl.ds(pl.multiple_of((i * grp + j) * c, c), c)
            cos, sin = cos_ref[rows, :], sin_ref[rows, :]
            q = rot(q_ref[rows, :], cos, sin)
            k = rot(k_ref[rows, :], cos, sin) * (RET_DK ** -0.5)
            v = v_ref[rows, :].astype(BF16)
            qs[rows, :] = q.astype(BF16)
            qd[rows, :] = jnp.concatenate([q * q_in, q * q_out], axis=1).astype(BF16)
            ks[rows, :] = k.astype(BF16)
            vs[rows, :] = v
            ks_f.append((k * k_in).T.astype(BF16))
            ks_b.append((k * k_out).T.astype(BF16))
            vs_.append(v)
        for j in range(grp):
            uf[i * grp + j] = jnp.dot(ks_f[j], vs_[j], preferred_element_type=F32)
            ub[i * grp + j] = jnp.dot(ks_b[j], vs_[j], preferred_element_type=F32)
        return carry

    lax.fori_loop(0, n_chunks // grp, prep, 0)

    def fwd_state(n, s):
        u = uf[n]
        uf[n] = s
        return s * chunk_decay + u

    lax.fori_loop(0, n_chunks, fwd_state, jnp.zeros((RET_DK, RET_DV), F32))

    def bwd_state(i, s):
        n = n_chunks - 1 - i
        u = ub[n]
        ub[n] = s
        return s * chunk_decay + u

    lax.fori_loop(0, n_chunks, bwd_state, jnp.zeros((RET_DK, RET_DV), F32))

    def out(i, carry):
        rows = [pl.ds(pl.multiple_of((i * grp + j) * c, c), c) for j in range(grp)]
        ss = [_dot_nt(qs[r, :], ks[r, :]) for r in rows]
        ss = [(s * dm_ref[...]).astype(BF16) for s in ss]
        states = [jnp.concatenate([uf[i * grp + j], ub[i * grp + j]], axis=0).astype(BF16) for j in range(grp)]
        os_ = [jnp.dot(s, vs[r, :], preferred_element_type=F32) for s, r in zip(ss, rows)]
        os_ = [o + jnp.dot(qd[r, :], st, preferred_element_type=F32) for o, r, st in zip(os_, rows, states)]
        for o, r in zip(os_, rows):
            mu = jnp.mean(o, axis=-1, keepdims=True)
            d = o - mu
            o = d * lax.rsqrt(jnp.mean(d * d, axis=-1, keepdims=True) + RMS_EPS)
            g = g_ref[r, :]
            o_ref[r, :] = g * _sigmoid(g) * o
        return carry

    lax.fori_loop(0, n_chunks // grp, out, 0)


def retention(proj, b, l, col0):
    cosf, sinf, dmat, sc = _ret_tables(l)
    cb = col0 // 128
    nh = RET_HEADS
    c = RET_CHUNK
    seq = lambda off: pl.BlockSpec((l, 128), lambda i, h: (i, cb + off * nh + h))
    const2 = pl.BlockSpec((l, 128), lambda i, h: (0, 0))
    return pl.pallas_call(
        _ret_body,
        grid=(b, nh),
        in_specs=[seq(0), seq(1), seq(2), seq(3), const2, const2,
                  pl.BlockSpec((None, c, c), lambda i, h: (h, 0, 0)),
                  pl.BlockSpec((None, 5, c, 128), lambda i, h: (h, 0, 0, 0))],
        out_specs=pl.BlockSpec((l, 128), lambda i, h: (i, h)),
        out_shape=jax.ShapeDtypeStruct((b * l, nh * RET_DV), F32),
        scratch_shapes=[pltpu.VMEM((l, 128), BF16), pltpu.VMEM((l, 256), BF16), pltpu.VMEM((l, 128), BF16),
                        pltpu.VMEM((l, 128), BF16), pltpu.VMEM((l // c, 128, 128), F32),
                        pltpu.VMEM((l // c, 128, 128), F32)],
        compiler_params=_cparams(("parallel", "parallel")),
        name="retention",
    )(proj, proj, proj, proj, cosf, sinf, dmat, sc)


GDN_ROWS = 512
HALO = 8


def _cumsum_rows(x, reverse):
    n = x.shape[0]
    row = lax.broadcasted_iota(jnp.int32, x.shape, 0)
    s = 1
    while s < n:
        if reverse:
            x = x + jnp.where(row < n - s, pltpu.roll(x, n - s, 0), 0.0)
        else:
            x = x + jnp.where(row >= s, pltpu.roll(x, s, 0), 0.0)
        s *= 2
    return x


def _gdn_local(insts):
    c = GDN_CHUNK
    row = lax.broadcasted_iota(jnp.int32, (c, c), 0)
    col = lax.broadcasted_iota(jnp.int32, (c, c), 1)
    pre = []
    for q, k, v, log_g, beta, d in insts:
        reverse = d == 1
        cum = _cumsum_rows(jnp.broadcast_to(log_g, (c, 128)), reverse)
        total = cum[0:1, :] if reverse else cum[c - 1:c, :]
        cum_sq = cum[:, :c]
        cum_row = jnp.sum(jnp.where(row == col, cum_sq, 0.0), axis=0, keepdims=True)
        incl = (row <= col) if reverse else (row >= col)
        strict = (row < col) if reverse else (row > col)
        decay = jnp.where(incl, jnp.exp(jnp.where(incl, cum_sq - cum_row, 0.0)), 0.0)
        pre.append((cum, total, decay, strict, k * beta, v * beta))
    qks = [_dot_nt(jnp.concatenate([q, p[4]], axis=0), k) for (q, k, *_), p in zip(insts, pre)]
    bps = [-jnp.where(p[3], qk[c:] * p[2], 0.0) for qk, p in zip(qks, pre)]
    rs = list(bps)
    for _ in range(5):
        bps = [_dot(bp, bp) for bp in bps]
        rs = [r + bp + _dot(r, bp) for r, bp in zip(rs, bps)]
    rhss = [jnp.concatenate([p[5], p[4] * jnp.exp(p[0])], axis=1) for p in pre]
    sols = [rhs + _dot(r, rhs) for r, rhs in zip(rs, rhss)]
    kdts, attns = [], []
    for (q, k, *_), p, qk in zip(insts, pre, qks):
        cum, total, decay = p[0], p[1], p[2]
        k_dec = k * jnp.exp(total - cum)
        kdts.append(jnp.concatenate([k_dec, jnp.zeros((c, 128), F32)], axis=0).T[:, :c])
        attns.append(qk[:c] * decay)
    kus = [_dot(kd_t, sol) for kd_t, sol in zip(kdts, sols)]
    aos = [_dot(attn, sol) for attn, sol in zip(attns, sols)]
    outs = []
    for (q, *_), p, ku, ao in zip(insts, pre, kus, aos):
        q_eff = q * jnp.exp(p[0]) - ao[:, GDN_DV:]
        outs.append((jnp.concatenate([ku[:, GDN_DV:], q_eff], axis=0), ku[:, :GDN_DV], ao[:, :GDN_DV], jnp.exp(p[1])))
    return outs


def _gdn_body(alog_ref, dtb_ref, q_ref, k_ref, v_ref, gate_ref, gab_ref, cwq_ref, cwk_ref, cwv_ref, onorm_ref,
              o_ref, pad, km_s, u_s, o0_s, gl_s, sf, sb):
    c = GDN_CHUNK
    l = q_ref.shape[0]
    h = pl.program_id(1)
    n_chunks = l // c
    per_block = GDN_ROWS // c

    zero_halo = jnp.zeros((HALO, 128), F32)
    for a, src in enumerate((q_ref, k_ref, v_ref)):
        pad[a, 0:HALO, :] = zero_halo
        pad[a, l + HALO:l + 2 * HALO, :] = zero_halo
        pad[a, HALO:l + HALO, :] = src[...]

    lane = lax.broadcasted_iota(jnp.int32, (1, 4), 1)
    a_row = jnp.where(lane == 0, alog_ref[0, h], alog_ref[1, h])
    dtb_row = jnp.where(lane == 0, dtb_ref[0, h], dtb_ref[1, h])
    neg_a_row = -jnp.exp(a_row)
    win_rows = GDN_ROWS + 2 * HALO

    def local_block(i, carry):
        s0 = pl.multiple_of(i * GDN_ROWS, GDN_ROWS)
        ys = []
        for a, cw_ref in enumerate((cwq_ref, cwk_ref, cwv_ref)):
            win = pad[a, pl.ds(s0, win_rows), :]
            acc = cw_ref[2:3, :] * win
            for j in (0, 1, 3, 4):
                acc = acc + cw_ref[j:j + 1, :] * pltpu.roll(win, (GDN_CONV // 2 - j) % win_rows, 0)
            y = acc[HALO:HALO + GDN_ROWS]
            y = y * _sigmoid(y)
            if a < 2:
                y = y * lax.rsqrt(jnp.sum(y * y, axis=-1, keepdims=True) + 1e-6)
            if a == 0:
                y = y * (GDN_DK ** -0.5)
            ys.append(y)
        ab = gab_ref[pl.ds(s0, GDN_ROWS), :]
        x = ab + dtb_row
        log_g = neg_a_row * (jnp.maximum(x, 0.0) + jnp.log(1.0 + jnp.exp(-jnp.abs(x))))
        beta = _sigmoid(ab)
        insts = []
        for j in range(per_block):
            sl = slice(j * c, (j + 1) * c)
            for d in range(2):
                insts.append((ys[0][sl], ys[1][sl], ys[2][sl], log_g[sl, d:d + 1], beta[sl, 2 + d:3 + d], d))
        outs = _gdn_local(insts)
        for idx, (km, u, o0, g_last) in enumerate(outs):
            m = (idx % 2) * n_chunks + i * per_block + idx // 2
            km_s[m] = km.astype(BF16)
            u_s[m] = u.astype(BF16)
            o0_s[m] = o0
            gl_s[m] = jnp.broadcast_to(g_last, (8, 128))
        return carry

    lax.fori_loop(0, l // GDN_ROWS, local_block, 0)

    sf[...] = jnp.zeros_like(sf)
    sb[...] = jnp.zeros_like(sb)

    def scan(i, carry):
        for d, s_ref in ((0, sf), (1, sb)):
            n = i if d == 0 else n_chunks - 1 - i
            m = d * n_chunks + n
            rows = pl.ds(pl.multiple_of(n * c, c) + HALO, c)
            s = s_ref[...]
            r = jnp.dot(km_s[m], s.astype(BF16), preferred_element_type=F32)
            pad[d, rows, :] = o0_s[m] + r[GDN_DK:]
            s_ref[...] = s * gl_s[m][0:1, :] - r[:GDN_DK] + u_s[m].astype(F32)
        return carry

    lax.fori_loop(0, n_chunks, scan, 0)

    def fin(i, carry):
        r0 = pl.multiple_of(i * GDN_ROWS, GDN_ROWS)
        o = pad[0, pl.ds(r0 + HALO, GDN_ROWS), :] + pad[1, pl.ds(r0 + HALO, GDN_ROWS), :]
        o = o * lax.rsqrt(jnp.mean(o * o, axis=-1, keepdims=True) + RMS_EPS) * onorm_ref[...]
        g = gate_ref[pl.ds(r0, GDN_ROWS), :]
        o_ref[pl.ds(r0, GDN_ROWS), :] = o * (g * _sigmoid(g))
        return carry

    lax.fori_loop(0, l // GDN_ROWS, fin, 0)


def gdn(proj, gabh, conv_w, a_log, dt_bias, onorm, b, l, col0):
    cb = col0 // 128
    nh = GDN_HEADS
    nc = l // GDN_CHUNK
    seq = lambda off: pl.BlockSpec((l, 128), lambda i, h, *_: (i, cb + off * nh + h))
    cw = lambda off: pl.BlockSpec((GDN_CONV, 128), lambda i, h, *_: (0, off * nh + h))
    grid_spec = pltpu.PrefetchScalarGridSpec(
        num_scalar_prefetch=2,
        grid=(b, nh),
        in_specs=[seq(0), seq(1), seq(2), seq(3),
                  pl.BlockSpec((None, l, 4), lambda i, h, *_: (h, i, 0)),
                  cw(0), cw(1), cw(2),
                  pl.BlockSpec((1, 128), lambda i, h, *_: (0, 0))],
        out_specs=pl.BlockSpec((l, 128), lambda i, h, *_: (i, h)),
        scratch_shapes=[pltpu.VMEM((3, l + 2 * HALO, 128), F32),
                        pltpu.VMEM((2 * nc, GDN_DK + GDN_CHUNK, GDN_DK), BF16),
                        pltpu.VMEM((2 * nc, GDN_DK, GDN_DV), BF16),
                        pltpu.VMEM((2 * nc, GDN_CHUNK, GDN_DV), F32),
                        pltpu.VMEM((2 * nc, 8, 128), F32),
                        pltpu.VMEM((GDN_DK, GDN_DV), F32), pltpu.VMEM((GDN_DK, GDN_DV), F32)],
    )
    return pl.pallas_call(
        _gdn_body,
        grid_spec=grid_spec,
        out_shape=jax.ShapeDtypeStruct((b * l, nh * GDN_DV), F32),
        compiler_params=_cparams(("parallel", "parallel")),
        name="gated_deltanet",
    )(a_log, dt_bias, proj, proj, proj, proj, gabh, conv_w, conv_w, conv_w, onorm[None, :])


HY_N1 = 64
HY_ROWS = 256


def _hyconv_body(x0_ref, x1_ref, v_ref, c0_ref, c1_ref, cv_ref, z_ref, x0c_ref, pad):
    l = x0_ref.shape[0]
    zero_halo = jnp.zeros((HALO, 128), F32)
    for a, src in enumerate((x0_ref, x1_ref, v_ref)):
        pad[a, 0:HALO, :] = zero_halo
        pad[a, l + HALO:l + 2 * HALO, :] = zero_halo
        pad[a, HALO:l + HALO, :] = src[...]
    win_rows = HY_ROWS + 2 * HALO

    def block(i, carry):
        s0 = pl.multiple_of(i * HY_ROWS, HY_ROWS)
        outs = []
        for a, cw_ref in enumerate((c0_ref, c1_ref, cv_ref)):
            win = pad[a, pl.ds(s0, win_rows), :]
            acc = cw_ref[1:2, :] * win
            acc = acc + cw_ref[0:1, :] * pltpu.roll(win, 1, 0)
            acc = acc + cw_ref[2:3, :] * pltpu.roll(win, win_rows - 1, 0)
            outs.append(acc[HALO:HALO + HY_ROWS])
        rows = pl.ds(s0, HY_ROWS)
        x0c_ref[rows, :] = outs[0]
        z_ref[rows, :] = outs[2] * outs[1]
        return carry

    lax.fori_loop(0, l // HY_ROWS, block, 0)


def _hy_conv(proj, conv_w, b, l, col0):
    cb = col0 // 128
    nc = HY_WIDTH // 128
    seq = lambda off: pl.BlockSpec((l, 128), lambda i, c: (i, cb + off * nc + c))
    cw = lambda off: pl.BlockSpec((HY_SHORT, 128), lambda i, c: (0, off * nc + c))
    out = pl.BlockSpec((l, 128), lambda i, c: (i, c))
    return pl.pallas_call(
        _hyconv_body,
        grid=(b, nc),
        in_specs=[seq(0), seq(1), seq(2), cw(0), cw(1), cw(2)],
        out_specs=[out, out],
        out_shape=[jax.ShapeDtypeStruct((b * l, HY_WIDTH), F32), jax.ShapeDtypeStruct((b * l, HY_WIDTH), F32)],
        scratch_shapes=[pltpu.VMEM((3, l + 2 * HALO, 128), F32)],
        compiler_params=_cparams(("parallel", "parallel")),
        name="hyena_short_conv",
    )(proj, proj, proj, conv_w, conv_w, conv_w)


def _hyfilt_body(feat_ref, w1_ref, b1_ref, w2_ref, b2_ref, w3_ref, b3_ref, fr_ref, wo_ref, dec_ref, o_ref, *, l):
    tm = feat_ref.shape[0]
    hdot = lambda a, b_: jnp.dot(a, b_, precision=HI, preferred_element_type=F32)
    feat = feat_ref[...]
    fr = fr_ref[...]
    h = jnp.sin(fr * (hdot(feat, w1_ref[...]) + b1_ref[...]))
    h = jnp.sin(fr * (hdot(h, w2_ref[...]) + b2_ref[...]))
    h = jnp.sin(fr * (hdot(h, w3_ref[...]) + b3_ref[...]))
    h = hdot(h, wo_ref[...]) * jnp.exp(-feat[:, 0:1] * jnp.abs(dec_ref[...]))
    m = pl.program_id(0) * tm + lax.broadcasted_iota(jnp.int32, (tm, 1), 0)
    o_ref[...] = jnp.where(m == l, 0.0, h)


def _hy_filter(l, w1, b1, w2, b2, w3, b3, freq, wout, decay, tm=512):
    n = 2 * l
    m = jnp.arange(n, dtype=jnp.int32)
    t = jnp.where(m < l, m, n - m).astype(F32)[:, None]
    t01 = t / (l - 1)
    bands = (HY_EMB - 1) // 2
    f = jnp.linspace(1e-4, bands - 1, bands, dtype=F32)[None, :]
    w = 2.0 * math.pi * t / l
    feat = jnp.concatenate([t01, jnp.cos(f * w), -jnp.sin(f * w)], axis=-1)
    const = lambda i: (0, 0)
    half = lambda i: (0, (i * tm) // l)
    vec = lambda x: x[None, :]
    return pl.pallas_call(
        functools.partial(_hyfilt_body, l=l),
        grid=(n // tm,),
        in_specs=[pl.BlockSpec((tm, HY_EMB), lambda i: (i, 0)),
                  pl.BlockSpec((HY_EMB, HY_FFN), const), pl.BlockSpec((1, HY_FFN), const),
                  pl.BlockSpec((HY_FFN, HY_FFN), const), pl.BlockSpec((1, HY_FFN), const),
                  pl.BlockSpec((HY_FFN, HY_FFN), const), pl.BlockSpec((1, HY_FFN), const),
                  pl.BlockSpec((1, HY_FFN), const),
                  pl.BlockSpec((HY_FFN, HY_WIDTH), half),
                  pl.BlockSpec((None, 1, HY_WIDTH), lambda i: ((i * tm) // l, 0, 0))],
        out_specs=pl.BlockSpec((tm, HY_WIDTH), lambda i: (i, 0)),
        out_shape=jax.ShapeDtypeStruct((n, HY_WIDTH), F32),
        compiler_params=_cparams(("parallel",)),
        name="hyena_filter",
    )(feat, w1, vec(b1), w2, vec(b2), w3, vec(b3), vec(freq), wout, decay[:, None, :])


def _dft_tables(l):
    n = 2 * l
    n1, n2 = HY_N1, n // HY_N1
    ang = lambda num, den: (2.0 * math.pi / den) * (num % den).astype(F32)
    i1 = jnp.arange(n1, dtype=jnp.int32)
    a1 = ang(i1[:, None] * i1[None, :], n1)
    fwd1 = jnp.concatenate([jnp.cos(a1), -jnp.sin(a1)], axis=0)
    a1h = a1[:, :n1 // 2].T
    inv1 = jnp.concatenate([jnp.cos(a1h), -jnp.sin(a1h)], axis=1) / n
    i2 = jnp.arange(n2, dtype=jnp.int32)
    freq = i1[:, None, None] + n1 * i2[None, :, None]
    a2 = ang(freq * i2[None, None, :], n)
    gr, gi = jnp.cos(a2), -jnp.sin(a2)
    fwd2 = jnp.concatenate([jnp.concatenate([gr, -gi], axis=2), jnp.concatenate([gi, gr], axis=2)], axis=1)
    inv2 = jnp.swapaxes(fwd2, 1, 2)
    return fwd1, inv1, fwd2, inv2


HY_TR = 8


def _dft1_body(x_ref, f_ref, o_ref, *, exact):
    k1, tr, c = x_ref.shape
    x = x_ref[...].reshape(k1 * tr, c)
    if exact:
        o = jnp.dot(f_ref[...], x, precision=HI, preferred_element_type=F32)
    else:
        o = jnp.dot(f_ref[...], x.astype(BF16), preferred_element_type=F32)
    o_ref[...] = o.astype(o_ref.dtype).reshape(o_ref.shape)


def _dft1(x4, mat, out_dtype, exact):
    bt, k1, n2, c = x4.shape
    n1 = mat.shape[0] // 2
    kron = jnp.kron(mat, jnp.eye(HY_TR, dtype=mat.dtype))
    return pl.pallas_call(
        functools.partial(_dft1_body, exact=exact),
        grid=(bt, n2 // HY_TR),
        in_specs=[pl.BlockSpec((None, k1, HY_TR, c), lambda i, j: (i, 0, j, 0)),
                  pl.BlockSpec((2 * n1 * HY_TR, k1 * HY_TR), lambda i, j: (0, 0))],
        out_specs=pl.BlockSpec((None, 2, n1, HY_TR, c), lambda i, j: (i, 0, 0, j, 0)),
        out_shape=jax.ShapeDtypeStruct((bt, 2, n1, n2, c), out_dtype),
        compiler_params=_cparams(("parallel", "parallel")),
        name="hyena_dft_stage1",
    )(x4, kron)


def _spec2_body(a_ref, g_ref, o_ref):
    n2 = a_ref.shape[1]
    a = a_ref[...].reshape(2 * n2, a_ref.shape[2])
    o = jnp.dot(g_ref[...], a, precision=HI, preferred_element_type=F32)
    o_ref[...] = o.reshape(o_ref.shape)


def _filter_spectrum(circ, fwd1, fwd2, l):
    n1, n2 = HY_N1, 2 * l // HY_N1
    a5 = _dft1(circ.reshape(1, n1, n2, HY_WIDTH), fwd1, F32, True)[0]
    return pl.pallas_call(
        _spec2_body,
        grid=(n1,),
        in_specs=[pl.BlockSpec((2, None, n2, HY_WIDTH), lambda i: (0, i, 0, 0)),
                  pl.BlockSpec((None, 2 * n2, 2 * n2), lambda i: (i, 0, 0))],
        out_specs=pl.BlockSpec((None, 2, n2, HY_WIDTH), lambda i: (i, 0, 0, 0)),
        out_shape=jax.ShapeDtypeStruct((n1, 2, n2, HY_WIDTH), F32),
        compiler_params=_cparams(("parallel",)),
        name="hyena_filter_spectrum",
    )(a5, fwd2)


def _hymid_body(a_ref, g_ref, gi_ref, h_ref, o_ref):
    bb, _, n2, c = a_ref.shape
    a = jnp.concatenate([a_ref[i].reshape(2 * n2, c) for i in range(bb)], axis=1)
    x = jnp.dot(g_ref[...], a, preferred_element_type=F32)
    hr = jnp.concatenate([h_ref[0]] * bb, axis=1)
    hi = jnp.concatenate([h_ref[1]] * bb, axis=1)
    xr, xi = x[:n2], x[n2:]
    y = jnp.concatenate([xr * hr - xi * hi, xr * hi + xi * hr], axis=0).astype(BF16)
    o = jnp.dot(gi_ref[...], y, preferred_element_type=F32).astype(o_ref.dtype)
    for i in range(bb):
        o_ref[i] = o[:, i * c:(i + 1) * c].reshape(2, n2, c)


def _hy_mid(a5, fwd2, inv2, spec):
    b, _, n1, n2, c = a5.shape
    bb = 8 if b % 8 == 0 else 4
    blk = pl.BlockSpec((bb, 2, None, n2, c), lambda f, i: (i, 0, f, 0, 0))
    mat = pl.BlockSpec((None, 2 * n2, 2 * n2), lambda f, i: (f, 0, 0))
    return pl.pallas_call(
        _hymid_body,
        grid=(n1, b // bb),
        in_specs=[blk, mat, mat, pl.BlockSpec((None, 2, n2, c), lambda f, i: (f, 0, 0, 0))],
        out_specs=blk,
        out_shape=jax.ShapeDtypeStruct(a5.shape, BF16),
        compiler_params=_cparams(("parallel", "parallel")),
        name="hyena_spectral_product",
    )(a5, fwd2, inv2, spec)


def _hyfin_body(b_ref, m_ref, z_ref, x0_ref, bias_ref, o_ref):
    k1, tr, c = z_ref.shape
    y = jnp.dot(m_ref[...], b_ref[...].reshape(m_ref.shape[1], c), preferred_element_type=F32).reshape(k1, tr, c)
    o_ref[...] = (y + z_ref[...] * bias_ref[...]) * x0_ref[...]


def _hy_final(bq5, inv1, z4, x04, bias):
    b, _, n1, n2, c = bq5.shape
    k1 = z4.shape[1]
    kron = jnp.kron(inv1, jnp.eye(HY_TR, dtype=inv1.dtype))
    row = pl.BlockSpec((None, k1, HY_TR, c), lambda i, j: (i, 0, j, 0))
    return pl.pallas_call(
        _hyfin_body,
        grid=(b, n2 // HY_TR),
        in_specs=[pl.BlockSpec((None, 2, n1, HY_TR, c), lambda i, j: (i, 0, 0, j, 0)),
                  pl.BlockSpec((k1 * HY_TR, 2 * n1 * HY_TR), lambda i, j: (0, 0)), row, row,
                  pl.BlockSpec((1, 1, c), lambda i, j: (0, 0, 0))],
        out_specs=row,
        out_shape=jax.ShapeDtypeStruct(z4.shape, F32),
        compiler_params=_cparams(("parallel", "parallel")),
        name="hyena_dft_final",
    )(bq5, kron, z4, x04, bias[None, None, :])


def hyena(proj, conv_w, w1, b1, w2, b2, w3, b3, freq, wout, decay, bias, b, l, col0):
    n1, n2, c = HY_N1, 2 * l // HY_N1, HY_WIDTH
    fwd1, inv1, fwd2, inv2 = _dft_tables(l)
    circ = _hy_filter(l, w1, b1, w2, b2, w3, b3, freq, wout, decay)
    spec = _filter_spectrum(circ, fwd1, fwd2, l)
    z, x0c = _hy_conv(proj, conv_w, b, l, col0)
    z4 = z.reshape(b, n1 // 2, n2, c)
    a5 = _dft1(z4, fwd1[:, :n1 // 2].astype(BF16), BF16, False)
    bq5 = _hy_mid(a5, fwd2.astype(BF16), inv2.astype(BF16), spec)
    out = _hy_final(bq5, inv1.astype(BF16), z4, x0c.reshape(b, n1 // 2, n2, c), bias)
    return out.reshape(b * l, c)


def _reorder_w_in(w):
    rest, gab, hy, gates = w[:, :4096], w[:, 4096:4112], w[:, 4112:5648], w[:, 5648:]
    pad = jnp.zeros((D_MODEL, D_PROJ - COL_GAB - 16), w.dtype)
    return jnp.concatenate([gates, rest, hy, gab, pad], axis=1).astype(BF16)


def _trunk(x3, p):
    b, l, _ = x3.shape
    t = b * l
    x = x3.reshape(t, D_MODEL)
    for i in range(DEPTH):
        proj = _proj(x, p['norm_mix'][i][None, :], p['w_in_r'][i])
        y_ret = retention(proj, b, l, COL_RET)
        gabh = jnp.transpose(proj[:, COL_GAB:COL_GAB + 4 * GDN_HEADS].reshape(t, 4, GDN_HEADS), (2, 0, 1))
        y_gdn = gdn(proj, gabh, p['gdn_conv_w'][i], p['gdn_A_log'][i], p['gdn_dt_bias'][i], p['gdn_onorm'][i],
                    b, l, COL_GDN)
        y_hy = hyena(proj, p['hy_conv_w'][i], p['hy_w1'][i], p['hy_b1'][i], p['hy_w2'][i], p['hy_b2'][i],
                     p['hy_w3'][i], p['hy_b3'][i], p['hy_freq'][i], p['hy_wout'][i], p['hy_decay'][i],
                     p['hy_bias'][i], b, l, COL_HY)
        x = _merge(x, proj, y_ret, y_gdn, y_hy, p['w_ret_o_b'][i], p['w_gdn_o_b'][i], p['w_hy_o_b'][i], p['w_o_b'][i])
        g = p['norm_ffn'][i][None, :]
        if i % 2 == 0:
            x = _ffn(x, g, p['dense_w_in_b'][i // 2], p['dense_w_out_b'][i // 2])
        else:
            x = _moe(x, g, p['moe_router_t'][i // 2], p['moe_w_in_b'][i // 2], p['moe_w_out_b'][i // 2])
    return _final_norm(x, p['norm_final'][None, :]).reshape(b, l, D_MODEL)


def kernel(x_prompt, x_sample, norm_mix, w_in, gdn_conv_w, gdn_A_log, gdn_dt_bias, gdn_onorm, hy_conv_w, hy_w1, hy_b1, hy_w2, hy_b2, hy_w3, hy_b3, hy_freq, hy_wout, hy_decay, hy_bias, w_ret_o, w_gdn_o, w_hy_o, w_o, norm_ffn, dense_w_in, dense_w_out, moe_router, moe_w_in, moe_w_out, norm_final):
    p = dict(norm_mix=norm_mix, gdn_conv_w=gdn_conv_w, gdn_A_log=gdn_A_log, gdn_dt_bias=gdn_dt_bias,
             gdn_onorm=gdn_onorm, hy_conv_w=hy_conv_w, hy_w1=hy_w1, hy_b1=hy_b1, hy_w2=hy_w2, hy_b2=hy_b2,
             hy_w3=hy_w3, hy_b3=hy_b3, hy_freq=hy_freq, hy_wout=hy_wout, hy_decay=hy_decay, hy_bias=hy_bias,
             norm_ffn=norm_ffn, norm_final=norm_final)
    p['w_in_r'] = jax.vmap(_reorder_w_in)(w_in)
    p['w_ret_o_b'] = w_ret_o.astype(BF16)
    p['w_gdn_o_b'] = w_gdn_o.astype(BF16)
    p['w_hy_o_b'] = w_hy_o.astype(BF16)
    p['w_o_b'] = w_o.astype(BF16)
    p['dense_w_in_b'] = dense_w_in.astype(BF16)
    p['dense_w_out_b'] = dense_w_out.astype(BF16)
    p['moe_router_t'] = jnp.swapaxes(moe_router, 1, 2)
    p['moe_w_in_b'] = moe_w_in
    p['moe_w_out_b'] = moe_w_out
    return (_trunk(x_prompt, p), _trunk(x_sample, p))
```
